```python
import jax, jax.numpy as jnp
from jax import lax
import numpy as np

D_MODEL = 1024
BATCH = 2
SEQ = 8192
DEPTH = 4
DEC_BATCH = 32
DEC_SEQ = 4
PAST_LEN = 8192
PAGE_SIZE = 128

N_MIXERS = 3
N_FOX = (DEPTH + 2) // 3
N_RWKV = (DEPTH + 1) // 3
N_POOLMIX = DEPTH // 3
HEAD_DIM = 64
N_HEADS = D_MODEL // HEAD_DIM
Q_BLOCK = 128
FORGET_BIAS_INIT = 2.0
RWKV_LORA_DECAY = 64
RWKV_LORA_A = 64
POOL_WINDOWS = (2, 4, 8, 16)
N_POOL_GROUPS = 4
POOL_GROUP = D_MODEL // N_POOL_GROUPS
POOL_MAX = 16
POOL_BUF = POOL_MAX - 1
RMS_EPS = 1e-6
GN_EPS = 64e-5
L2_EPS_SQ = 1e-24
F32 = jnp.float32

kernel_name = 'fox_rwkv7_pool_hybrid_step'


def rmsnorm(x, w):
    xf = x.astype(F32)
    y = xf * lax.rsqrt(jnp.mean(xf * xf, axis=-1, keepdims=True) + RMS_EPS)
    return (y * w.astype(F32)).astype(x.dtype)


def gated_output(o, gate, w_out):
    return jnp.einsum('btd,de->bte', o * jax.nn.silu(gate), w_out)


def fox_project(xn, w_in, b_f, q_gain, k_gain):
    B, T, D = xn.shape
    proj = jnp.einsum('btd,de->bte', xn, w_in)
    heads = lambda z: z.reshape(B, T, N_HEADS, HEAD_DIM)
    q = rmsnorm(heads(proj[..., :D]), q_gain)
    k = rmsnorm(heads(proj[..., D:2 * D]), k_gain)
    v = heads(proj[..., 2 * D:3 * D])
    gate = proj[..., 3 * D:4 * D]
    logf = jax.nn.log_sigmoid((proj[..., 4 * D:] + b_f).astype(F32))
    return q, k, v, gate, logf


def fox_prompt_attention(q, k, v, logf):
    B, T, H, Dh = q.shape
    nb = T // Q_BLOCK
    scale = Dh ** -0.5
    cum = jnp.cumsum(logf, axis=1).transpose(0, 2, 1)
    pos = jnp.arange(T)
    q_blocks = q.reshape(B, nb, Q_BLOCK, H, Dh).transpose(1, 0, 2, 3, 4)
    c_blocks = cum.reshape(B, H, nb, Q_BLOCK).transpose(2, 0, 1, 3)
    p_blocks = pos.reshape(nb, Q_BLOCK)

    def block(args):
        qb, cb, pb = args
        s = jnp.einsum('bqhd,bkhd->bhqk', qb, k, preferred_element_type=F32) * scale
        s = s + cb[..., :, None] - cum[:, :, None, :]
        s = jnp.where(pb[:, None] >= pos[None, :], s, -jnp.inf)
        p = jax.nn.softmax(s, axis=-1)
        return jnp.einsum('bhqk,bkhd->bqhd', p.astype(v.dtype), v)

    o = lax.map(block, (q_blocks, c_blocks, p_blocks))
    return o.transpose(1, 0, 2, 3, 4).reshape(B, T, H * Dh)


def fox_sample_attention(q, k, v, logf, k_past, v_past, logf_past):
    B, S, H, Dh = q.shape
    P = k_past.shape[1]
    scale = Dh ** -0.5
    cum_past = jnp.cumsum(logf_past.astype(F32), axis=1)
    cum_new = cum_past[:, -1:] + jnp.cumsum(logf, axis=1)
    cp = cum_past.transpose(0, 2, 1)
    cn = cum_new.transpose(0, 2, 1)
    s_past = jnp.einsum('bqhd,bkhd->bhqk', q, k_past, preferred_element_type=F32) * scale
    s_past = s_past + cn[..., :, None] - cp[..., None, :]
    s_new = jnp.einsum('bqhd,bkhd->bhqk', q, k, preferred_element_type=F32) * scale
    s_new = s_new + cn[..., :, None] - cn[..., None, :]
    causal = jnp.tril(jnp.ones((S, S), dtype=bool))
    s_new = jnp.where(causal, s_new, -jnp.inf)
    p = jax.nn.softmax(jnp.concatenate([s_past, s_new], axis=-1), axis=-1).astype(v.dtype)
    o = (jnp.einsum('bhqk,bkhd->bqhd', p[..., :P], v_past)
         + jnp.einsum('bhqk,bkhd->bqhd', p[..., P:], v))
    return o.reshape(B, S, H * Dh)


def wkv_scan(S0, r, decay, k, v, kk, a):
    def step(S, inp):
        r_t, w_t, k_t, v_t, kk_t, a_t = inp
        sa = jnp.einsum('bhvk,bhk->bhv', S, -kk_t)
        S = (S * w_t[:, :, None, :] + sa[..., None] * (kk_t * a_t)[:, :, None, :]
             + v_t[..., None] * k_t[:, :, None, :])
        return S, jnp.einsum('bhvk,bhk->bhv', S, r_t)

    xs = tuple(t.astype(F32).transpose(1, 0, 2, 3) for t in (r, decay, k, v, kk, a))
    S, ys = lax.scan(step, S0.astype(F32), xs)
    return ys.transpose(1, 0, 2, 3), S


def rwkv_time_mix(xn, x_last, S0, mu, w_rkvg, w0, w1, w2, a0, a1, a2, k_k, k_a, r_k, ln_w, ln_b, w_out):
    B, T, D = xn.shape
    xf = xn.astype(F32)
    x_prev = jnp.concatenate([x_last.astype(F32)[:, None], xf[:, :-1]], axis=1)
    xs = xf[:, :, None, :] + (x_prev - xf)[:, :, None, :] * mu
    rkvg = jnp.einsum('btjd,jde->btje', xs[:, :, :4], w_rkvg)
    r, k, v, g = rkvg[:, :, 0], rkvg[:, :, 1], rkvg[:, :, 2], rkvg[:, :, 3]
    w_log = -jax.nn.softplus(-(w0 + jnp.tanh(xs[:, :, 4] @ w1) @ w2)) - 0.5
    decay = jnp.exp(-jnp.exp(w_log))
    a = jax.nn.sigmoid(a0 + (xs[:, :, 5] @ a1) @ a2)
    heads = lambda z: z.reshape(B, T, N_HEADS, HEAD_DIM)
    kk = heads(k * k_k)
    kk = kk * lax.rsqrt(jnp.maximum(jnp.sum(kk * kk, axis=-1, keepdims=True), L2_EPS_SQ))
    k = k * (1.0 + (a - 1.0) * k_a)
    r_h, k_h, v_h = heads(r), heads(k), heads(v)
    y, S = wkv_scan(S0, r_h, heads(decay), k_h, v_h, kk, heads(a))
    mean = jnp.mean(y, axis=-1, keepdims=True)
    var = jnp.mean(jnp.square(y - mean), axis=-1, keepdims=True)
    yn = ((y - mean) * lax.rsqrt(var + GN_EPS)).reshape(B, T, D) * ln_w + ln_b
    bonus = (jnp.sum(r_h * k_h * r_k, axis=-1, keepdims=True) * v_h).reshape(B, T, D)
    out = gated_output(yn + bonus, g, w_out)
    return out, S, xn[:, -1]


def multiscale_pool(u_ctx, pos, w_grp, scale):
    B, L, D = u_ctx.shape
    uf = u_ctx.astype(F32)
    cs = jnp.concatenate([jnp.zeros((B, POOL_MAX, D), F32), jnp.cumsum(uf, axis=1)], axis=1)
    diffs = []
    for gi, w in enumerate(POOL_WINDOWS):
        sl = slice(gi * POOL_GROUP, (gi + 1) * POOL_GROUP)
        win_sum = cs[:, POOL_MAX:, sl] - cs[:, POOL_MAX - w:POOL_MAX - w + L, sl]
        count = jnp.minimum(pos + 1, w).astype(F32)[None, :, None]
        diffs.append(win_sum / count - uf[..., sl])
    d = jnp.stack(diffs, axis=2)
    mixed = jnp.einsum('blgc,gce->blge', d, w_grp.astype(F32)).reshape(B, L, D)
    return mixed * scale


def setup_inputs(seed: int = 0) -> dict:
    key = jax.random.key(seed)
    D, H, Dh = D_MODEL, N_HEADS, HEAD_DIM
    n_pages = PAST_LEN // PAGE_SIZE
    n_used = DEC_BATCH * n_pages
    n_pool_pages = n_used + n_used // 4
    kid = [0]

    def nk():
        kid[0] += 1
        return jax.random.fold_in(key, kid[0])

    def nrm(shape, s):
        return s * jax.random.normal(nk(), shape, F32)

    inp = {}
    inp['x_prompt'] = nrm((BATCH, SEQ, D), 1.0)
    inp['x_sample'] = nrm((DEC_BATCH, DEC_SEQ, D), 1.0)
    inp['cache_k'] = nrm((N_FOX, n_pool_pages, PAGE_SIZE, H, Dh), 1.0)
    inp['cache_v'] = nrm((N_FOX, n_pool_pages, PAGE_SIZE, H, Dh), 1.0)
    inp['cache_logf'] = jax.nn.log_sigmoid(FORGET_BIAS_INIT + nrm((N_FOX, n_pool_pages, PAGE_SIZE, H), 1.0))
    inp['page_table'] = jax.random.permutation(nk(), n_pool_pages)[:n_used].reshape(DEC_BATCH, n_pages).astype(jnp.int32)
    inp['state_wkv'] = nrm((N_RWKV, DEC_BATCH, H, Dh, Dh), 0.3)
    inp['state_shift'] = nrm((N_RWKV, DEC_BATCH, D), 1.0)
    inp['state_pool'] = nrm((N_POOLMIX, DEC_BATCH, POOL_BUF, D), 1.0)
    inp['norm_w'] = 1.0 + nrm((DEPTH, D), 0.1)
    inp['fox_w_in'] = nrm((N_FOX, D, 4 * D + H), D ** -0.5)
    inp['fox_b_f'] = FORGET_BIAS_INIT + nrm((N_FOX, H), 0.1)
    inp['fox_q_gain'] = 1.0 + nrm((N_FOX, Dh), 0.1)
    inp['fox_k_gain'] = 1.0 + nrm((N_FOX, Dh), 0.1)
    inp['fox_w_out'] = nrm((N_FOX, D, D), D ** -0.5)
    inp['rwkv_mu'] = jax.random.uniform(nk(), (N_RWKV, 6, D), F32)
    inp['rwkv_w_rkvg'] = nrm((N_RWKV, 4, D, D), D ** -0.5)
    inp['rwkv_w0'] = nrm((N_RWKV, D), 0.5)
    inp['rwkv_w1'] = nrm((N_RWKV, D, RWKV_LORA_DECAY), D ** -0.5)
    inp['rwkv_w2'] = nrm((N_RWKV, RWKV_LORA_DECAY, D), 0.1)
    inp['rwkv_a0'] = nrm((N_RWKV, D), 0.1)
    inp['rwkv_a1'] = nrm((N_RWKV, D, RWKV_LORA_A), D ** -0.5)
    inp['rwkv_a2'] = nrm((N_RWKV, RWKV_LORA_A, D), 0.1)
    inp['rwkv_k_k'] = 0.85 + nrm((N_RWKV, D), 0.02)
    inp['rwkv_k_a'] = 1.0 + nrm((N_RWKV, D), 0.02)
    inp['rwkv_r_k'] = nrm((N_RWKV, H, Dh), 0.1)
    inp['rwkv_ln_w'] = 1.0 + nrm((N_RWKV, D), 0.1)
    inp['rwkv_ln_b'] = nrm((N_RWKV, D), 0.01)
    inp['rwkv_w_out'] = nrm((N_RWKV, D, D), D ** -0.5)
    inp['pool_w_in'] = nrm((N_POOLMIX, D, 2 * D), D ** -0.5)
    inp['pool_w_grp'] = nrm((N_POOLMIX, N_POOL_GROUPS, POOL_GROUP, POOL_GROUP), POOL_GROUP ** -0.5)
    inp['pool_scale'] = 1.0 + nrm((N_POOLMIX, D), 0.1)
    inp['pool_w_out'] = nrm((N_POOLMIX, D, D), D ** -0.5)
    return inp


def reference(x_prompt, x_sample, cache_k, cache_v, cache_logf, page_table, state_wkv, state_shift, state_pool,
              norm_w, fox_w_in, fox_b_f, fox_q_gain, fox_k_gain, fox_w_out,
              rwkv_mu, rwkv_w_rkvg, rwkv_w0, rwkv_w1, rwkv_w2, rwkv_a0, rwkv_a1, rwkv_a2,
              rwkv_k_k, rwkv_k_a, rwkv_r_k, rwkv_ln_w, rwkv_ln_b, rwkv_w_out,
              pool_w_in, pool_w_grp, pool_scale, pool_w_out):
    D = D_MODEL
    B_p, T_p, _ = x_prompt.shape
    B_s, S_s, _ = x_sample.shape
    past = page_table.shape[1] * PAGE_SIZE
    pos_prompt = jnp.arange(T_p)
    pos_sample_ctx = past - POOL_BUF + jnp.arange(POOL_BUF + S_s)
    y_p, y_s = x_prompt, x_sample
    kp, vp, lp, ks, vs, ls = [], [], [], [], [], []
    wkv_p, sh_p, wkv_s, sh_s = [], [], [], []
    pool_p, pool_s = [], []
    for i in range(DEPTH):
        kind, j = i % N_MIXERS, i // N_MIXERS
        xn_p = rmsnorm(y_p, norm_w[i])
        xn_s = rmsnorm(y_s, norm_w[i])
        if kind == 0:
            q, k, v, g, lf = fox_project(xn_p, fox_w_in[j], fox_b_f[j], fox_q_gain[j], fox_k_gain[j])
            out_p = gated_output(fox_prompt_attention(q, k, v, lf), g, fox_w_out[j])
            kp.append(k); vp.append(v); lp.append(lf)
            q, k, v, g, lf = fox_project(xn_s, fox_w_in[j], fox_b_f[j], fox_q_gain[j], fox_k_gain[j])
            k_past = cache_k[j, page_table].reshape(B_s, past, N_HEADS, HEAD_DIM)
            v_past = cache_v[j, page_table].reshape(B_s, past, N_HEADS, HEAD_DIM)
            lf_past = cache_logf[j, page_table].reshape(B_s, past, N_HEADS)
            o = fox_sample_attention(q, k, v, lf, k_past, v_past, lf_past)
            out_s = gated_output(o, g, fox_w_out[j])
            ks.append(k); vs.append(v); ls.append(lf)
        elif kind == 1:
            prm = (rwkv_mu[j], rwkv_w_rkvg[j], rwkv_w0[j], rwkv_w1[j], rwkv_w2[j], rwkv_a0[j], rwkv_a1[j],
                   rwkv_a2[j], rwkv_k_k[j], rwkv_k_a[j], rwkv_r_k[j], rwkv_ln_w[j], rwkv_ln_b[j], rwkv_w_out[j])
            zero_last = jnp.zeros((B_p, D), xn_p.dtype)
            zero_state = jnp.zeros((B_p, N_HEADS, HEAD_DIM, HEAD_DIM), F32)
            out_p, S_new, last = rwkv_time_mix(xn_p, zero_last, zero_state, *prm)
            wkv_p.append(S_new); sh_p.append(last)
            out_s, S_new, last = rwkv_time_mix(xn_s, state_shift[j], state_wkv[j], *prm)
            wkv_s.append(S_new); sh_s.append(last)
        else:
            proj = jnp.einsum('btd,de->bte', xn_p, pool_w_in[j])
            u, g = proj[..., :D], proj[..., D:]
            mixed = multiscale_pool(u, pos_prompt, pool_w_grp[j], pool_scale[j]).astype(g.dtype)
            out_p = gated_output(mixed, g, pool_w_out[j])
            pool_p.append(u[:, -POOL_BUF:])
            proj = jnp.einsum('btd,de->bte', xn_s, pool_w_in[j])
            u, g = proj[..., :D], proj[..., D:]
            u_ctx = jnp.concatenate([state_pool[j].astype(u.dtype), u], axis=1)
            mixed = multiscale_pool(u_ctx, pos_sample_ctx, pool_w_grp[j], pool_scale[j])[:, POOL_BUF:].astype(g.dtype)
            out_s = gated_output(mixed, g, pool_w_out[j])
            pool_s.append(u_ctx[:, -POOL_BUF:])
        y_p = y_p + out_p.astype(y_p.dtype)
        y_s = y_s + out_s.astype(y_s.dtype)
    new_k_prompt = jnp.stack(kp, 0)
    new_v_prompt = jnp.stack(vp, 0)
    new_logf_prompt = jnp.stack(lp, 0)
    new_k_sample = jnp.stack(ks, 0)
    new_v_sample = jnp.stack(vs, 0)
    new_logf_sample = jnp.stack(ls, 0)
    new_wkv_prompt = jnp.stack(wkv_p, 0)
    new_shift_prompt = jnp.stack(sh_p, 0)
    new_wkv_sample = jnp.stack(wkv_s, 0)
    new_shift_sample = jnp.stack(sh_s, 0)
    new_pool_prompt = jnp.stack(pool_p, 0)
    new_pool_sample = jnp.stack(pool_s, 0)
    return (y_p, y_s, new_k_prompt, new_v_prompt, new_logf_prompt, new_k_sample, new_v_sample, new_logf_sample,
            new_wkv_prompt, new_shift_prompt, new_wkv_sample, new_shift_sample, new_pool_prompt, new_pool_sample)
```

```python
import functools

import jax
import jax.numpy as jnp
import numpy as np
from jax import lax
from jax.experimental import pallas as pl
from jax.experimental.pallas import tpu as pltpu

F32 = jnp.float32
BF16 = jnp.bfloat16

D_MODEL = 1024
HEAD_DIM = 64
N_HEADS = D_MODEL // HEAD_DIM
PAGE_SIZE = 128
POOL_WINDOWS = (2, 4, 8, 16)
POOL_GROUP = D_MODEL // len(POOL_WINDOWS)
POOL_HIST = 16
RMS_EPS = 1e-6
GN_EPS = 64e-5
L2_EPS_SQ = 1e-24
ATTN_SCALE = HEAD_DIM ** -0.5

LANES = 128
SUBLANES = 8
K_HI = HEAD_DIM // SUBLANES
VMEM_LIMIT = 56 * 1024 * 1024


def _cparams(*sem):
    return pltpu.CompilerParams(dimension_semantics=sem, vmem_limit_bytes=VMEM_LIMIT)


def _const_spec(shape):
    nd = len(shape)
    return pl.BlockSpec(shape, lambda *_: (0,) * nd, pipeline_mode=pl.Buffered(1))


def _rmsnorm_rows(x, w):
    ms = jnp.mean(x * x, axis=-1, keepdims=True)
    return x * lax.rsqrt(ms + RMS_EPS) * w


def _sigmoid(x):
    return 1.0 / (1.0 + jnp.exp(-x))


def _bdot(a, b):
    return jnp.dot(a.astype(BF16), b, preferred_element_type=F32)


def _dot3(x, m):
    hi = x.astype(BF16)
    r1 = x - hi.astype(F32)
    mid = r1.astype(BF16)
    lo = (r1 - mid.astype(F32)).astype(BF16)
    return (jnp.dot(hi, m, preferred_element_type=F32)
            + jnp.dot(mid, m, preferred_element_type=F32)
            + jnp.dot(lo, m, preferred_element_type=F32))


def _lane_group_sum(x):
    x = x + pltpu.roll(x, 16, axis=x.ndim - 1)
    x = x + pltpu.roll(x, 32, axis=x.ndim - 1)
    return x + pltpu.roll(x, 64, axis=x.ndim - 1)


def _fox_proj_kernel(x_ref, nw_ref, w_ref, wf_ref, bf_ref, qg_ref, kg_ref, gm_ref,
                     qb_ref, k_ref, kb_ref, v_ref, vb_ref, g_ref, lf_ref):
    xb = _rmsnorm_rows(x_ref[...], nw_ref[...]).astype(BF16)

    def headnorm(z, gain):
        ms = _bdot(z * z, gm_ref[...]) * (1.0 / HEAD_DIM)
        return z * lax.rsqrt(ms + RMS_EPS) * gain

    d = D_MODEL
    q = headnorm(jnp.dot(xb, w_ref[:, 0:d], preferred_element_type=F32), qg_ref[...])
    qb_ref[...] = (q * ATTN_SCALE).astype(BF16)
    k = headnorm(jnp.dot(xb, w_ref[:, d:2 * d], preferred_element_type=F32), kg_ref[...])
    k_ref[...] = k
    kb_ref[...] = k.astype(BF16)
    v = jnp.dot(xb, w_ref[:, 2 * d:3 * d], preferred_element_type=F32)
    v_ref[...] = v
    vb_ref[...] = v.astype(BF16)
    g_ref[...] = jnp.dot(xb, w_ref[:, 3 * d:4 * d], preferred_element_type=F32)
    fl = jnp.dot(xb, wf_ref[...], preferred_element_type=F32) + bf_ref[...]
    lf_ref[...] = -(jnp.maximum(-fl, 0.0) + jnp.log(1.0 + jnp.exp(-jnp.abs(fl))))


def _fox_proj(x2d, nw, w_main, w_f, b_f, qg, kg, gmat, tm):
    m = x2d.shape[0]
    d = D_MODEL
    row = lambda n: pl.BlockSpec((tm, n), lambda i: (i, 0))
    outs = (jax.ShapeDtypeStruct((m, d), BF16), jax.ShapeDtypeStruct((m, d), F32),
            jax.ShapeDtypeStruct((m, d), BF16), jax.ShapeDtypeStruct((m, d), F32),
            jax.ShapeDtypeStruct((m, d), BF16), jax.ShapeDtypeStruct((m, d), F32),
            jax.ShapeDtypeStruct((m, LANES), F32))
    return pl.pallas_call(
        _fox_proj_kernel,
        grid=(m // tm,),
        in_specs=[row(d), _const_spec((1, d)), _const_spec((d, 4 * d)), _const_spec((d, LANES)),
                  _const_spec((1, LANES)), _const_spec((1, d)), _const_spec((1, d)),
                  _const_spec((d, d))],
        out_specs=[row(d)] * 6 + [row(LANES)],
        out_shape=outs,
        compiler_params=_cparams("arbitrary"),
        name="fox_proj",
    )(x2d, nw, w_main, w_f, b_f, qg, kg, gmat)


def _cumsum_kernel(x_ref, tri_ref, o_ref, carry_ref, *, nchunk):
    @pl.when(pl.program_id(1) == 0)
    def _():
        carry_ref[...] = jnp.zeros_like(carry_ref)

    carry = carry_ref[...]
    upper = tri_ref[0]
    ones = tri_ref[1]
    for c in range(nchunk):
        x = x_ref[0, :, c * LANES:(c + 1) * LANES]
        o_ref[0, :, c * LANES:(c + 1) * LANES] = carry + _dot3(x, upper)
        carry = carry + _dot3(x, ones)
    carry_ref[...] = carry


def _tri_mats():
    i = np.arange(LANES)
    upper = (i[:, None] <= i[None, :]).astype(np.float32)
    return jnp.asarray(np.stack([upper, np.ones_like(upper)]), dtype=BF16)


def _cumsum_time(x_bht, tri):
    b, h, t = x_bht.shape
    tc = min(t, 1024)
    return pl.pallas_call(
        functools.partial(_cumsum_kernel, nchunk=tc // LANES),
        grid=(b, t // tc),
        in_specs=[pl.BlockSpec((1, h, tc), lambda i, j: (i, 0, j)), _const_spec((2, LANES, LANES))],
        out_specs=pl.BlockSpec((1, h, tc), lambda i, j: (i, 0, j)),
        out_shape=jax.ShapeDtypeStruct((b, h, t), F32),
        scratch_shapes=[pltpu.VMEM((h, LANES), F32)],
        compiler_params=_cparams("arbitrary", "arbitrary"),
        name="logf_cumsum",
    )(x_bht, tri)


def _fox_attn_kernel(q_ref, k_ref, v_ref, c_ref, o_ref, *, tq):
    qi = pl.program_id(2)
    q2 = q_ref[0]
    lane = lax.broadcasted_iota(jnp.int32, (tq, LANES), 1)
    zero = jnp.zeros_like(q2)
    qh = (jnp.where(lane < HEAD_DIM, q2, zero), jnp.where(lane >= HEAD_DIM, q2, zero))
    row = lax.broadcasted_iota(jnp.int32, (tq, tq), 0)
    col = lax.broadcasted_iota(jnp.int32, (tq, tq), 1)

    def block(kj, carry, diagonal):
        start = pl.multiple_of(kj * tq, tq)
        k2 = k_ref[0, pl.ds(start, tq), :]
        v2 = v_ref[0, pl.ds(start, tq), :]
        cj = c_ref[0, 0, kj]
        out = []
        for hh in range(2):
            m, l, acc = carry[hh]
            s = lax.dot_general(qh[hh], k2, (((1,), (1,)), ((), ())), preferred_element_type=F32)
            s = s - cj[hh:hh + 1, :]
            if diagonal:
                s = jnp.where(row >= col, s, -jnp.inf)
            m_new = jnp.maximum(m, jnp.max(s, axis=-1, keepdims=True))
            alpha = jnp.exp(m - m_new)
            p = jnp.exp(s - m_new)
            l = alpha * l + jnp.sum(p, axis=-1, keepdims=True)
            acc = alpha * acc + jnp.dot(p.astype(BF16), v2, preferred_element_type=F32)
            out.append((m_new, l, acc))
        return tuple(out)

    init = tuple((jnp.full((tq, 1), -jnp.inf, F32), jnp.zeros((tq, 1), F32),
                  jnp.zeros((tq, LANES), F32)) for _ in range(2))
    carry = lax.fori_loop(0, qi, lambda kj, c: block(kj, c, False), init)
    (_, l0, a0), (_, l1, a1) = block(qi, carry, True)
    o_ref[0] = jnp.where(lane < HEAD_DIM, a0 / l0, a1 / l1)


def _fox_attention(qb, kb, vb, cum_bht, tq):
    b, t, d = qb.shape
    hp = N_HEADS // 2
    nt = t // tq
    cum5 = cum_bht.reshape(b, hp, 2, nt, tq).transpose(0, 1, 3, 2, 4)
    return pl.pallas_call(
        functools.partial(_fox_attn_kernel, tq=tq),
        grid=(b, hp, nt),
        in_specs=[pl.BlockSpec((1, tq, LANES), lambda i, h, q: (i, q, h)),
                  pl.BlockSpec((1, t, LANES), lambda i, h, q: (i, 0, h)),
                  pl.BlockSpec((1, t, LANES), lambda i, h, q: (i, 0, h)),
                  pl.BlockSpec((1, 1, nt, 2, tq), lambda i, h, q: (i, h, 0, 0, 0))],
        out_specs=pl.BlockSpec((1, tq, LANES), lambda i, h, q: (i, q, h)),
        out_shape=jax.ShapeDtypeStruct((b, t, d), F32),
        compiler_params=_cparams("arbitrary", "arbitrary", "arbitrary"),
        name="fox_prompt_attention",
    )(qb, kb, vb, cum5)


def _fox_decode_kernel(pt_ref, q_ref, k_ref, v_ref, lf_ref, kn_ref, vn_ref, lfn_ref, tri_ref,
                       o_ref, m_sc, l_sc, acc_sc, carry_sc, *, n_q):
    del pt_ref
    p = pl.program_id(1)
    rows = n_q * N_HEADS

    @pl.when(p == 0)
    def _():
        m_sc[...] = jnp.full_like(m_sc, -jnp.inf)
        l_sc[...] = jnp.zeros_like(l_sc)
        acc_sc[...] = jnp.zeros_like(acc_sc)
        carry_sc[...] = jnp.zeros_like(carry_sc)

    q = q_ref[0]

    def update(kpage, vpage, lft, mask):
        cpos = carry_sc[...] + _dot3(lft, tri_ref[0])
        carry_sc[...] = carry_sc[...] + _dot3(lft, tri_ref[1])
        s = lax.dot_general(q, kpage, (((1,), (1,)), ((), ())), preferred_element_type=F32)
        s = s - jnp.concatenate([cpos] * n_q, axis=0)
        if mask is not None:
            s = jnp.where(mask, s, -jnp.inf)
        m = m_sc[...]
        m_new = jnp.maximum(m, jnp.max(s, axis=-1, keepdims=True))
        alpha = jnp.exp(m - m_new)
        pr = jnp.exp(s - m_new)
        l_sc[...] = alpha * l_sc[...] + jnp.sum(pr, axis=-1, keepdims=True)
        acc_sc[...] = alpha * acc_sc[...] + jnp.dot(pr.astype(BF16), vpage,
                                                    preferred_element_type=F32)
        m_sc[...] = m_new

    update(k_ref[0, 0].astype(BF16), v_ref[0, 0].astype(BF16), lf_ref[0, 0], None)

    @pl.when(p == pl.num_programs(1) - 1)
    def _():
        r = lax.broadcasted_iota(jnp.int32, (rows, PAGE_SIZE), 0)
        j = lax.broadcasted_iota(jnp.int32, (rows, PAGE_SIZE), 1)
        update(kn_ref[0], vn_ref[0], lfn_ref[0], j <= r // N_HEADS)
        o = acc_sc[...] / l_sc[...]
        rr = lax.broadcasted_iota(jnp.int32, (rows, D_MODEL), 0)
        cc = lax.broadcasted_iota(jnp.int32, (rows, D_MODEL), 1)
        o = jnp.where(rr % N_HEADS == cc // HEAD_DIM, o, 0.0)
        o_ref[0] = jnp.concatenate(
            [jnp.sum(o[N_HEADS * t:N_HEADS * (t + 1)], axis=0, keepdims=True) for t in range(n_q)],
            axis=0)


def _fox_decode(layer, page_table, q_bd, cache_k4, cache_v4, lft_all, kn, vn, lfn, tri):
    b, rows, d = q_bd.shape
    n_q = rows // N_HEADS
    n_pages = page_table.shape[1]
    page = lambda i, p, pt: (layer, pt[i, p], 0, 0)
    per_b = lambda i, p, pt: (i, 0, 0)
    grid_spec = pltpu.PrefetchScalarGridSpec(
        num_scalar_prefetch=1,
        grid=(b, n_pages),
        in_specs=[pl.BlockSpec((1, rows, d), per_b),
                  pl.BlockSpec((1, 1, PAGE_SIZE, d), page),
                  pl.BlockSpec((1, 1, PAGE_SIZE, d), page),
                  pl.BlockSpec((1, 1, N_HEADS, PAGE_SIZE), page),
                  pl.BlockSpec((1, PAGE_SIZE, d), per_b),
                  pl.BlockSpec((1, PAGE_SIZE, d), per_b),
                  pl.BlockSpec((1, N_HEADS, PAGE_SIZE), per_b),
                  pl.BlockSpec((2, LANES, LANES), lambda i, p, pt: (0, 0, 0))],
        out_specs=pl.BlockSpec((1, n_q, d), per_b),
        scratch_shapes=[pltpu.VMEM((rows, 1), F32), pltpu.VMEM((rows, 1), F32),
                        pltpu.VMEM((rows, d), F32), pltpu.VMEM((N_HEADS, LANES), F32)])
    return pl.pallas_call(
        functools.partial(_fox_decode_kernel, n_q=n_q),
        grid_spec=grid_spec,
        out_shape=jax.ShapeDtypeStruct((b, n_q, d), F32),
        compiler_params=_cparams("arbitrary", "arbitrary"),
        name="fox_sample_attention",
    )(page_table, q_bd, cache_k4, cache_v4, lft_all, kn, vn, lfn, tri)


def _gated_out_kernel(o_ref, g_ref, res_ref, w_ref, y_ref):
    g = g_ref[...]
    z = o_ref[...] * (g * _sigmoid(g))
    y_ref[...] = res_ref[...] + _bdot(z, w_ref[...])


def _gated_out(o2d, g2d, res2d, w_out, tm):
    m, d = o2d.shape
    row = pl.BlockSpec((tm, d), lambda i: (i, 0))
    return pl.pallas_call(
        _gated_out_kernel,
        grid=(m // tm,),
        in_specs=[row, row, row, _const_spec((d, d))],
        out_specs=row,
        out_shape=jax.ShapeDtypeStruct((m, d), F32),
        compiler_params=_cparams("arbitrary"),
        name="gated_out_proj",
    )(o2d, g2d, res2d, w_out)


def _rwkv_pre_kernel(x_ref, xlast_ref, nw_ref, mu_ref, wr_ref, wk_ref, wg_ref, w1_ref, w2_ref,
                     a1_ref, a2_ref, vec_ref,
                     r_ref, dec_ref, k_ref, kk_ref, beta_ref, xv_ref, g_ref, last_ref,
                     carry_ref, *, tm, t_last):
    @pl.when(pl.program_id(1) == 0)
    def _():
        carry_ref[...] = xlast_ref[0]

    xn = _rmsnorm_rows(x_ref[0], nw_ref[...])
    rowi = lax.broadcasted_iota(jnp.int32, (tm, D_MODEL), 0)
    prev = jnp.where(rowi == 0, carry_ref[...], pltpu.roll(xn, 1, axis=0))
    carry_ref[...] = xn[tm - 1:tm, :]
    last_ref[0] = xn[t_last:t_last + 1, :]
    dx = prev - xn
    mix = lambda j: xn + dx * mu_ref[j:j + 1, :]

    w0, a0, k_k, k_a = (vec_ref[i:i + 1, :] for i in range(4))
    r = _bdot(mix(0), wr_ref[...])
    k = _bdot(mix(1), wk_ref[...])
    xv_ref[0] = mix(2).astype(BF16)
    g_ref[0] = _bdot(mix(3), wg_ref[...])
    wl = w0 + _bdot(jnp.tanh(_bdot(mix(4), w1_ref[...])), w2_ref[...])
    w_log = -(jnp.maximum(-wl, 0.0) + jnp.log(1.0 + jnp.exp(-jnp.abs(wl)))) - 0.5
    dec_ref[0] = jnp.exp(-jnp.exp(w_log))
    a = _sigmoid(a0 + _bdot(_bdot(mix(5), a1_ref[...]), a2_ref[...]))
    kk = k * k_k
    sq = kk * kk
    tot = sq[:, 0:LANES]
    for c in range(1, K_HI):
        tot = tot + sq[:, c * LANES:(c + 1) * LANES]
    inv = lax.rsqrt(jnp.maximum(_lane_group_sum(tot), L2_EPS_SQ))
    kkn = kk * jnp.concatenate([inv] * K_HI, axis=1)
    r_ref[0] = r
    k_ref[0] = k * (1.0 + (a - 1.0) * k_a)
    kk_ref[0] = kkn
    beta_ref[0] = kkn * a


def _rwkv_pre(x3, x_last, nw, mu8, wr, wk, wg, w1, w2, a1, a2, vecs, tm, t_last):
    b, t, d = x3.shape
    tile = pl.BlockSpec((1, tm, d), lambda i, j: (i, j, 0))
    per_b = pl.BlockSpec((1, 1, d), lambda i, j: (i, 0, 0))
    f32o = jax.ShapeDtypeStruct((b, t, d), F32)
    return pl.pallas_call(
        functools.partial(_rwkv_pre_kernel, tm=tm, t_last=t_last),
        grid=(b, t // tm),
        in_specs=[tile, per_b, _const_spec((1, d)), _const_spec((8, d)),
                  _const_spec((d, d)), _const_spec((d, d)), _const_spec((d, d)),
                  _const_spec((d, LANES)), _const_spec((LANES, d)),
                  _const_spec((d, LANES)), _const_spec((LANES, d)), _const_spec((8, d))],
        out_specs=[tile] * 7 + [per_b],
        out_shape=(f32o, f32o, f32o, f32o, f32o, jax.ShapeDtypeStruct((b, t, d), BF16), f32o,
                   jax.ShapeDtypeStruct((b, 1, d), F32)),
        scratch_shapes=[pltpu.VMEM((1, d), F32)],
        compiler_params=_cparams("arbitrary", "arbitrary"),
        name="rwkv_pre",
    )(x3, x_last, nw, mu8, wr, wk, wg, w1, w2, a1, a2, vecs)


def _matmul_kernel(a_ref, b_ref, o_ref):
    o_ref[...] = jnp.dot(a_ref[...], b_ref[...], preferred_element_type=F32)


def _matmul_bf16(a, bmat, tm, tn):
    m, k = a.shape
    n = bmat.shape[1]
    return pl.pallas_call(
        _matmul_kernel,
        grid=(m // tm, n // tn),
        in_specs=[pl.BlockSpec((tm, k), lambda i, j: (i, 0)),
                  pl.BlockSpec((k, tn), lambda i, j: (0, j))],
        out_specs=pl.BlockSpec((tm, tn), lambda i, j: (i, j)),
        out_shape=jax.ShapeDtypeStruct((m, n), F32),
        compiler_params=_cparams("arbitrary", "arbitrary"),
        name="matmul_bf16",
    )(a, bmat)


def _wkv_kernel(r_ref, dec_ref, k_ref, kk_ref, beta_ref, vc_ref, s0_ref, lnw_ref, lnb_ref, rk_ref,
                z_ref, sf_ref, s_sc, *, nb, tb):
    @pl.when(pl.program_id(1) == 0)
    def _():
        s_sc[...] = s0_ref[...]

    def rows_total(x):
        return _lane_group_sum(jnp.sum(x, axis=0, keepdims=True))

    def step(t, carry):
        for b in range(nb):
            r = r_ref[b, t]
            w = dec_ref[b, t]
            km = k_ref[b, t]
            beta = beta_ref[b, t]
            vc = vc_ref[b, t]
            wr = w * r
            acc_sa = jnp.zeros((HEAD_DIM, LANES), F32)
            acc_y = jnp.zeros((HEAD_DIM, LANES), F32)
            for c in range(K_HI):
                s = s_sc[b, c]
                acc_sa = acc_sa + s * kk_ref[b, t, c:c + 1, :]
                acc_y = acc_y + s * wr[c:c + 1, :]
            sa = -_lane_group_sum(acc_sa)
            y = _lane_group_sum(acc_y) + sa * rows_total(beta * r) + vc * rows_total(km * r)
            for c in range(K_HI):
                s_sc[b, c] = (s_sc[b, c] * w[c:c + 1, :] + sa * beta[c:c + 1, :]
                              + vc * km[c:c + 1, :])
            mean = jnp.sum(y, axis=0, keepdims=True) * (1.0 / HEAD_DIM)
            dy = y - mean
            var = jnp.sum(dy * dy, axis=0, keepdims=True) * (1.0 / HEAD_DIM)
            yn = dy * lax.rsqrt(var + GN_EPS)
            bonus = rows_total(r * km * rk_ref[...])
            z_ref[b, t] = yn * lnw_ref[...] + lnb_ref[...] + vc * bonus
        return carry

    lax.fori_loop(0, tb, step, 0)

    @pl.when(pl.program_id(1) == pl.num_programs(1) - 1)
    def _():
        sf_ref[...] = s_sc[...]


def _wkv_scan(r4, dec4, k4, kk4, beta4, vc4, s0, lnw, lnb, rk, nb, tb):
    b, t = r4.shape[:2]
    rowf = pl.BlockSpec((nb, tb, K_HI, LANES), lambda i, j: (i, j, 0, 0))
    colf = pl.BlockSpec((nb, tb, HEAD_DIM, LANES), lambda i, j: (i, j, 0, 0))
    st = pl.BlockSpec((nb, K_HI, HEAD_DIM, LANES), lambda i, j: (i, 0, 0, 0))
    return pl.pallas_call(
        functools.partial(_wkv_kernel, nb=nb, tb=tb),
        grid=(b // nb, t // tb),
        in_specs=[rowf] * 5 + [colf, st, _const_spec((HEAD_DIM, LANES)),
                               _const_spec((HEAD_DIM, LANES)), _const_spec((K_HI, LANES))],
        out_specs=[colf, st],
        out_shape=(jax.ShapeDtypeStruct((b, t, HEAD_DIM, LANES), F32),
                   jax.ShapeDtypeStruct((b, K_HI, HEAD_DIM, LANES), F32)),
        scratch_shapes=[pltpu.VMEM((nb, K_HI, HEAD_DIM, LANES), F32)],
        compiler_params=_cparams("arbitrary", "arbitrary"),
        name="wkv_scan",
    )(r4, dec4, k4, kk4, beta4, vc4, s0, lnw, lnb, rk)


def _pool_kernel(x_ref, hist_ref, nw_ref, win_ref, wgrp_ref, scale_ref, wout_ref,
                 y_ref, tail_ref, ext_ref, *, tm, t_valid, pos0):
    j = pl.program_id(1)

    @pl.when(j == 0)
    def _():
        ext_ref[0:POOL_HIST, :] = hist_ref[0]

    x = x_ref[0]
    xb = _rmsnorm_rows(x, nw_ref[...]).astype(BF16)
    u = jnp.dot(xb, win_ref[:, 0:D_MODEL], preferred_element_type=F32)
    g = jnp.dot(xb, win_ref[:, D_MODEL:2 * D_MODEL], preferred_element_type=F32)
    ext_ref[POOL_HIST:POOL_HIST + tm, :] = u
    pos = pos0 + j * tm + lax.broadcasted_iota(jnp.int32, (tm, POOL_GROUP), 0)
    mixed = []
    for gi, w in enumerate(POOL_WINDOWS):
        lo, hi = gi * POOL_GROUP, (gi + 1) * POOL_GROUP
        win = u[:, lo:hi]
        for dlt in range(1, w):
            win = win + ext_ref[POOL_HIST - dlt:POOL_HIST - dlt + tm, lo:hi]
        cnt = jnp.minimum(pos + 1, w).astype(F32)
        diff = win / cnt - u[:, lo:hi]
        mixed.append(_bdot(diff, wgrp_ref[gi]))
    mix = jnp.concatenate(mixed, axis=1) * scale_ref[...]
    z = mix * (g * _sigmoid(g))
    y_ref[0] = x + _bdot(z, wout_ref[...])
    tail = ext_ref[t_valid:t_valid + POOL_HIST, :]
    tail_ref[0] = tail
    ext_ref[0:POOL_HIST, :] = tail


def _pool_mixer(x3, hist, nw, w_in, w_grp, scale, w_out, tm, t_valid, pos0):
    b, t, d = x3.shape
    tile = pl.BlockSpec((1, tm, d), lambda i, j: (i, j, 0))
    per_b = pl.BlockSpec((1, POOL_HIST, d), lambda i, j: (i, 0, 0))
    return pl.pallas_call(
        functools.partial(_pool_kernel, tm=tm, t_valid=t_valid, pos0=pos0),
        grid=(b, t // tm),
        in_specs=[tile, per_b, _const_spec((1, d)), _const_spec((d, 2 * d)),
                  _const_spec((len(POOL_WINDOWS), POOL_GROUP, POOL_GROUP)),
                  _const_spec((1, d)), _const_spec((d, d))],
        out_specs=[tile, per_b],
        out_shape=(jax.ShapeDtypeStruct((b, t, d), F32),
                   jax.ShapeDtypeStruct((b, POOL_HIST, d), F32)),
        scratch_shapes=[pltpu.VMEM((POOL_HIST + tm, d), F32)],
        compiler_params=_cparams("arbitrary", "arbitrary"),
        name="pool_mixer",
    )(x3, hist, nw, w_in, w_grp, scale, w_out)


def _key_perm():
    n = np.arange(D_MODEL)
    kc, kl, h = n // LANES, (n % LANES) // N_HEADS, n % N_HEADS
    return h * HEAD_DIM + kc * SUBLANES + kl


def _value_rep_perm():
    n = np.arange(HEAD_DIM * LANES)
    return (n % N_HEADS) * HEAD_DIM + n // LANES


def _pad_rows(x3, rows):
    b, t, d = x3.shape
    return jnp.concatenate([x3, jnp.zeros((b, rows - t, d), x3.dtype)], axis=1)


def _state_to_tiles(s):
    b = s.shape[0]
    s = s.reshape(b, N_HEADS, HEAD_DIM, K_HI, SUBLANES)
    return s.transpose(0, 3, 2, 4, 1).reshape(b, K_HI, HEAD_DIM, LANES)


def _tiles_to_state(s):
    b = s.shape[0]
    s = s.reshape(b, K_HI, HEAD_DIM, SUBLANES, N_HEADS)
    return s.transpose(0, 4, 2, 1, 3).reshape(b, N_HEADS, HEAD_DIM, HEAD_DIM)


def _head_col_tile(vec):
    return jnp.tile(vec.reshape(N_HEADS, HEAD_DIM).T, (1, SUBLANES))


def _fox_layer(j, y_p, y_s, nw, cache_k4, cache_v4, lft_all, page_table, fox_w_in, fox_b_f,
               fox_q_gain, fox_k_gain, fox_w_out, tri):
    d, h = D_MODEL, N_HEADS
    bp, tp, _ = y_p.shape
    bs, ts, _ = y_s.shape
    w_in = fox_w_in[j]
    w_main = w_in[:, :4 * d].astype(BF16)
    w_f = jnp.pad(w_in[:, 4 * d:], ((0, 0), (0, LANES - h))).astype(BF16)
    b_f = jnp.pad(fox_b_f[j], (0, LANES - h)).reshape(1, LANES)
    qg = jnp.tile(fox_q_gain[j], h).reshape(1, d)
    kg = jnp.tile(fox_k_gain[j], h).reshape(1, d)
    gmat = jnp.asarray(np.kron(np.eye(h), np.ones((HEAD_DIM, HEAD_DIM))), dtype=BF16)
    w_out = fox_w_out[j].astype(BF16)

    xp = y_p.reshape(bp * tp, d)
    qb, k, kb, v, vb, g, lf = _fox_proj(xp, nw, w_main, w_f, b_f, qg, kg, gmat, tm=256)
    lf_p = lf[:, :h].reshape(bp, tp, h)
    cum = _cumsum_time(lf_p.transpose(0, 2, 1), tri)
    sh = (bp, tp, d)
    o = _fox_attention(qb.reshape(sh), kb.reshape(sh), vb.reshape(sh), cum, tq=min(tp, 512))
    y_p_new = _gated_out(o.reshape(bp * tp, d), g, xp, w_out, tm=512).reshape(bp, tp, d)
    new_p = (k.reshape(bp, tp, h, HEAD_DIM), v.reshape(bp, tp, h, HEAD_DIM), lf_p)

    xs = y_s.reshape(bs * ts, d)
    qb, k, kb, v, vb, g, lf = _fox_proj(xs, nw, w_main, w_f, b_f, qg, kg, gmat, tm=bs * ts)
    lf_s = lf[:, :h].reshape(bs, ts, h)
    head_of_lane = np.arange(d) // HEAD_DIM
    head_mask = jnp.asarray(head_of_lane[None, :] == np.arange(h)[:, None])
    q_bd = jnp.where(head_mask[None, None], qb.reshape(bs, ts, 1, d), jnp.zeros((), BF16))
    q_bd = q_bd.reshape(bs, ts * h, d)
    kn = _pad_rows(kb.reshape(bs, ts, d), PAGE_SIZE)
    vn = _pad_rows(vb.reshape(bs, ts, d), PAGE_SIZE)
    lfn = jnp.pad(lf_s.transpose(0, 2, 1), ((0, 0), (0, 0), (0, PAGE_SIZE - ts)))
    o = _fox_decode(j, page_table, q_bd, cache_k4, cache_v4, lft_all, kn, vn, lfn, tri)
    y_s_new = _gated_out(o.reshape(bs * ts, d), g, xs, w_out, tm=bs * ts).reshape(bs, ts, d)
    new_s = (k.reshape(bs, ts, h, HEAD_DIM), v.reshape(bs, ts, h, HEAD_DIM), lf_s)
    return y_p_new, y_s_new, new_p, new_s


def _rwkv_group(y3, x_last, s0_tiles, nw, prm, tm, t_valid, nb, tb):
    b, t, d = y3.shape
    (mu8, wr, wk, wv_rep, wg, w1, w2, a1, a2, vecs, rk, lnw, lnb, w_out) = prm
    r, dec, k, kk, beta, xv, g, last = _rwkv_pre(y3, x_last, nw, mu8, wr, wk, wg, w1, w2, a1, a2,
                                                 vecs, tm=tm, t_last=(t_valid - 1) % tm)
    m = b * t
    vc = _matmul_bf16(xv.reshape(m, d), wv_rep, tm=min(m, 512), tn=2048)
    rowf = lambda z: z[:, :t_valid].reshape(b, t_valid, K_HI, LANES)
    vc4 = vc.reshape(b, t, HEAD_DIM, LANES)[:, :t_valid]
    z, s_fin = _wkv_scan(rowf(r), rowf(dec), rowf(k), rowf(kk), rowf(beta), vc4, s0_tiles,
                         lnw, lnb, rk, nb=nb, tb=tb)
    z = z[..., :N_HEADS].transpose(0, 1, 3, 2).reshape(b, t_valid, d)
    if t_valid != t:
        z = _pad_rows(z, t)
    tmo = min(m, 512)
    y_new = _gated_out(z.reshape(m, d), g.reshape(m, d), y3.reshape(m, d), w_out, tm=tmo)
    return y_new.reshape(b, t, d), s_fin, last[:, 0]


def _rwkv_params(j, rwkv_mu, rwkv_w_rkvg, rwkv_w0, rwkv_w1, rwkv_w2, rwkv_a0, rwkv_a1, rwkv_a2,
                 rwkv_k_k, rwkv_k_a, rwkv_r_k, rwkv_ln_w, rwkv_ln_b, rwkv_w_out):
    d = D_MODEL
    perm = _key_perm()
    lora = rwkv_w1.shape[-1]
    mu8 = jnp.pad(rwkv_mu[j], ((0, 2), (0, 0)))
    wr = rwkv_w_rkvg[j, 0][:, perm].astype(BF16)
    wk = rwkv_w_rkvg[j, 1][:, perm].astype(BF16)
    wv_rep = rwkv_w_rkvg[j, 2][:, _value_rep_perm()].astype(BF16)
    wg = rwkv_w_rkvg[j, 3].astype(BF16)
    w1 = jnp.pad(rwkv_w1[j], ((0, 0), (0, LANES - lora))).astype(BF16)
    w2 = jnp.pad(rwkv_w2[j][:, perm], ((0, LANES - lora), (0, 0))).astype(BF16)
    a1 = jnp.pad(rwkv_a1[j], ((0, 0), (0, LANES - lora))).astype(BF16)
    a2 = jnp.pad(rwkv_a2[j][:, perm], ((0, LANES - lora), (0, 0))).astype(BF16)
    vecs = jnp.stack([rwkv_w0[j][perm], rwkv_a0[j][perm], rwkv_k_k[j][perm], rwkv_k_a[j][perm]])
    vecs = jnp.pad(vecs, ((0, 4), (0, 0)))
    rk = rwkv_r_k[j].reshape(d)[perm].reshape(K_HI, LANES)
    lnw = _head_col_tile(rwkv_ln_w[j])
    lnb = _head_col_tile(rwkv_ln_b[j])
    return (mu8, wr, wk, wv_rep, wg, w1, w2, a1, a2, vecs, rk, lnw, lnb, rwkv_w_out[j].astype(BF16))


def kernel(x_prompt, x_sample, cache_k, cache_v, cache_logf, page_table, state_wkv, state_shift,
           state_pool, norm_w, fox_w_in, fox_b_f, fox_q_gain, fox_k_gain, fox_w_out,
           rwkv_mu, rwkv_w_rkvg, rwkv_w0, rwkv_w1, rwkv_w2, rwkv_a0, rwkv_a1, rwkv_a2,
           rwkv_k_k, rwkv_k_a, rwkv_r_k, rwkv_ln_w, rwkv_ln_b, rwkv_w_out,
           pool_w_in, pool_w_grp, pool_scale, pool_w_out):
    d = D_MODEL
    depth = norm_w.shape[0]
    bp, tp, _ = x_prompt.shape
    bs, ts, _ = x_sample.shape
    past = page_table.shape[1] * PAGE_SIZE
    n_fox, n_pool_pages = cache_k.shape[:2]
    cache_k4 = cache_k.reshape(n_fox, n_pool_pages, PAGE_SIZE, d)
    cache_v4 = cache_v.reshape(n_fox, n_pool_pages, PAGE_SIZE, d)
    lft_all = cache_logf.transpose(0, 1, 3, 2)
    tri = _tri_mats()
    ts_pad = SUBLANES * pl.cdiv(ts, SUBLANES)

    y_p, y_s = x_prompt, x_sample
    kp, vp, lp, ks, vs, ls = [], [], [], [], [], []
    wkv_p, sh_p, wkv_s, sh_s, pool_p, pool_s = [], [], [], [], [], []
    for i in range(depth):
        kind, j = i % 3, i // 3
        nw = norm_w[i].reshape(1, d)
        if kind == 0:
            y_p, y_s, new_p, new_s = _fox_layer(
                j, y_p, y_s, nw, cache_k4, cache_v4, lft_all, page_table, fox_w_in, fox_b_f,
                fox_q_gain, fox_k_gain, fox_w_out, tri)
            kp.append(new_p[0]); vp.append(new_p[1]); lp.append(new_p[2])
            ks.append(new_s[0]); vs.append(new_s[1]); ls.append(new_s[2])
        elif kind == 1:
            prm = _rwkv_params(j, rwkv_mu, rwkv_w_rkvg, rwkv_w0, rwkv_w1, rwkv_w2, rwkv_a0,
                               rwkv_a1, rwkv_a2, rwkv_k_k, rwkv_k_a, rwkv_r_k, rwkv_ln_w,
                               rwkv_ln_b, rwkv_w_out)
            zero_state = jnp.zeros((bp, K_HI, HEAD_DIM, LANES), F32)
            y_p, s_fin, last = _rwkv_group(y_p, jnp.zeros((bp, 1, d), F32), zero_state, nw, prm,
                                           tm=min(tp, 256), t_valid=tp, nb=bp, tb=min(tp, 64))
            wkv_p.append(_tiles_to_state(s_fin)); sh_p.append(last)
            y_s_pad, s_fin, last = _rwkv_group(
                _pad_rows(y_s, ts_pad), state_shift[j].reshape(bs, 1, d),
                _state_to_tiles(state_wkv[j]), nw, prm, tm=ts_pad, t_valid=ts, nb=2, tb=ts)
            y_s = y_s_pad[:, :ts]
            wkv_s.append(_tiles_to_state(s_fin)); sh_s.append(last)
        else:
            w_in = pool_w_in[j].astype(BF16)
            w_grp = pool_w_grp[j].astype(BF16)
            scale = pool_scale[j].reshape(1, d)
            w_out = pool_w_out[j].astype(BF16)
            y_p, tail = _pool_mixer(y_p, jnp.zeros((bp, POOL_HIST, d), F32), nw, w_in, w_grp,
                                    scale, w_out, tm=min(tp, 256), t_valid=min(tp, 256), pos0=0)
            pool_p.append(tail[:, 1:])
            hist = jnp.pad(state_pool[j], ((0, 0), (1, 0), (0, 0)))
            y_s_pad, tail = _pool_mixer(_pad_rows(y_s, ts_pad), hist, nw, w_in, w_grp, scale,
                                        w_out, tm=ts_pad, t_valid=ts, pos0=past)
            y_s = y_s_pad[:, :ts]
            pool_s.append(tail[:, 1:])
    st = lambda xs: jnp.stack(xs, 0)
    return (y_p, y_s, st(kp), st(vp), st(lp), st(ks), st(vs), st(ls),
            st(wkv_p), st(sh_p), st(wkv_s), st(sh_s), st(pool_p), st(pool_s))
```

```python
import functools

import jax
import jax.numpy as jnp
import numpy as np
from jax import lax
from jax.experimental import pallas as pl
from jax.experimental.pallas import tpu as pltpu

F32 = jnp.float32
BF16 = jnp.bfloat16

D_MODEL = 1024
HEAD_DIM = 64
N_HEADS = D_MODEL // HEAD_DIM
PAGE_SIZE = 128
POOL_WINDOWS = (2, 4, 8, 16)
POOL_GROUP = D_MODEL // len(POOL_WINDOWS)
POOL_HIST = 16
RMS_EPS = 1e-6
GN_EPS = 64e-5
L2_EPS_SQ = 1e-24
ATTN_SCALE = HEAD_DIM ** -0.5
LOG2E = 1.4426950408889634

LANES = 128
SUBLANES = 8
K_HI = HEAD_DIM // SUBLANES
VMEM_LIMIT = 56 * 1024 * 1024
DECODE_PAGES_PER_STEP = 4


def _cparams(*sem):
    return pltpu.CompilerParams(dimension_semantics=sem, vmem_limit_bytes=VMEM_LIMIT)


def _const_spec(shape):
    nd = len(shape)
    return pl.BlockSpec(shape, lambda *_: (0,) * nd, pipeline_mode=pl.Buffered(1))


def _rmsnorm_rows(x, w):
    ms = jnp.mean(x * x, axis=-1, keepdims=True)
    return x * lax.rsqrt(ms + RMS_EPS) * w


def _sigmoid(x):
    return 1.0 / (1.0 + jnp.exp(-x))


def _bdot(a, b):
    return jnp.dot(a.astype(BF16), b, preferred_element_type=F32)


def _dot3(x, m):
    hi = x.astype(BF16)
    r1 = x - hi.astype(F32)
    mid = r1.astype(BF16)
    lo = (r1 - mid.astype(F32)).astype(BF16)
    return (jnp.dot(hi, m, preferred_element_type=F32)
            + jnp.dot(mid, m, preferred_element_type=F32)
            + jnp.dot(lo, m, preferred_element_type=F32))


def _lane_group_sum(x):
    x = x + pltpu.roll(x, 16, axis=x.ndim - 1)
    x = x + pltpu.roll(x, 32, axis=x.ndim - 1)
    return x + pltpu.roll(x, 64, axis=x.ndim - 1)


def _fox_proj_kernel(x_ref, nw_ref, w_ref, wf_ref, bf_ref, qg_ref, kg_ref, gm_ref,
                     qb_ref, k_ref, kb_ref, v_ref, vb_ref, g_ref, lf_ref):
    xb = _rmsnorm_rows(x_ref[...], nw_ref[...]).astype(BF16)

    def headnorm(z, gain):
        ms = _bdot(z * z, gm_ref[...]) * (1.0 / HEAD_DIM)
        return z * lax.rsqrt(ms + RMS_EPS) * gain

    d = D_MODEL
    q = headnorm(jnp.dot(xb, w_ref[:, 0:d], preferred_element_type=F32), qg_ref[...])
    qb_ref[...] = (q * (ATTN_SCALE * LOG2E)).astype(BF16)
    k = headnorm(jnp.dot(xb, w_ref[:, d:2 * d], preferred_element_type=F32), kg_ref[...])
    k_ref[...] = k
    kb_ref[...] = k.astype(BF16)
    v = jnp.dot(xb, w_ref[:, 2 * d:3 * d], preferred_element_type=F32)
    v_ref[...] = v
    vb_ref[...] = v.astype(BF16)
    g_ref[...] = jnp.dot(xb, w_ref[:, 3 * d:4 * d], preferred_element_type=F32)
    fl = jnp.dot(xb, wf_ref[...], preferred_element_type=F32) + bf_ref[...]
    lf_ref[...] = -(jnp.maximum(-fl, 0.0) + jnp.log(1.0 + jnp.exp(-jnp.abs(fl))))


def _fox_proj(x2d, nw, w_main, w_f, b_f, qg, kg, gmat, tm):
    m = x2d.shape[0]
    d = D_MODEL
    row = lambda n: pl.BlockSpec((tm, n), lambda i: (i, 0))
    outs = (jax.ShapeDtypeStruct((m, d), BF16), jax.ShapeDtypeStruct((m, d), F32),
            jax.ShapeDtypeStruct((m, d), BF16), jax.ShapeDtypeStruct((m, d), F32),
            jax.ShapeDtypeStruct((m, d), BF16), jax.ShapeDtypeStruct((m, d), F32),
            jax.ShapeDtypeStruct((m, LANES), F32))
    return pl.pallas_call(
        _fox_proj_kernel,
        grid=(m // tm,),
        in_specs=[row(d), _const_spec((1, d)), _const_spec((d, 4 * d)), _const_spec((d, LANES)),
                  _const_spec((1, LANES)), _const_spec((1, d)), _const_spec((1, d)),
                  _const_spec((d, d))],
        out_specs=[row(d)] * 6 + [row(LANES)],
        out_shape=outs,
        compiler_params=_cparams("arbitrary"),
        name="fox_proj",
    )(x2d, nw, w_main, w_f, b_f, qg, kg, gmat)


def _cumsum_kernel(x_ref, tri_ref, o_ref, carry_ref, *, nchunk):
    @pl.when(pl.program_id(1) == 0)
    def _():
        carry_ref[...] = jnp.zeros_like(carry_ref)

    carry = carry_ref[...]
    upper = tri_ref[0]
    ones = tri_ref[1]
    for c in range(nchunk):
        x = x_ref[0, :, c * LANES:(c + 1) * LANES]
        o_ref[0, :, c * LANES:(c + 1) * LANES] = (carry + _dot3(x, upper)) * LOG2E
        carry = carry + _dot3(x, ones)
    carry_ref[...] = carry


def _tri_mats():
    i = np.arange(LANES)
    upper = (i[:, None] <= i[None, :]).astype(np.float32)
    return jnp.asarray(np.stack([upper, np.ones_like(upper)]), dtype=BF16)


def _cumsum_time(x_bht, tri):
    b, h, t = x_bht.shape
    tc = min(t, 1024)
    return pl.pallas_call(
        functools.partial(_cumsum_kernel, nchunk=tc // LANES),
        grid=(b, t // tc),
        in_specs=[pl.BlockSpec((1, h, tc), lambda i, j: (i, 0, j)), _const_spec((2, LANES, LANES))],
        out_specs=pl.BlockSpec((1, h, tc), lambda i, j: (i, 0, j)),
        out_shape=jax.ShapeDtypeStruct((b, h, t), F32),
        scratch_shapes=[pltpu.VMEM((h, LANES), F32)],
        compiler_params=_cparams("arbitrary", "arbitrary"),
        name="logf_cumsum",
    )(x_bht, tri)


def _fox_attn_kernel(q_ref, k_ref, v_ref, c_ref, o_ref, s0_sc, s1_sc, m_sc, acc_sc, *, tq):
    qi = pl.program_id(2)
    q2 = q_ref[0]
    lane = lax.broadcasted_iota(jnp.int32, (tq, LANES), 1)
    zero = jnp.zeros_like(q2)
    qh = (jnp.where(lane < HEAD_DIM, q2, zero), jnp.where(lane >= HEAD_DIM, q2, zero))

    def score(kj, dst):
        k2 = k_ref[0, pl.ds(pl.multiple_of(kj * tq, tq), tq), :]
        for hh in range(2):
            dst[hh] = lax.dot_general(qh[hh], k2, (((1,), (1,)), ((), ())),
                                      preferred_element_type=F32)

    def consume(kj, src, diagonal):
        v2 = v_ref[0, pl.ds(pl.multiple_of(kj * tq, tq), tq), :]
        one = jnp.ones_like(v2)
        vh = (jnp.where(lane < HEAD_DIM, v2, one), jnp.where(lane >= HEAD_DIM, v2, one))
        cj = c_ref[0, 0, kj]
        for hh in range(2):
            s = src[hh] - cj[hh:hh + 1, :]
            if diagonal:
                row = lax.broadcasted_iota(jnp.int32, (tq, tq), 0)
                col = lax.broadcasted_iota(jnp.int32, (tq, tq), 1)
                s = jnp.where(row >= col, s, -jnp.inf)
            part = s[:, 0:LANES]
            for c in range(1, tq // LANES):
                part = jnp.maximum(part, s[:, c * LANES:(c + 1) * LANES])
            m = m_sc[hh]
            m_new = jnp.maximum(m, jnp.max(part, axis=-1, keepdims=True))
            p = jnp.exp2(s - jnp.concatenate([m_new] * (tq // LANES), axis=1))
            acc_sc[hh] = jnp.exp2(m - m_new) * acc_sc[hh] + jnp.dot(
                p.astype(BF16), vh[hh], preferred_element_type=F32)
            m_sc[hh] = m_new

    m_sc[...] = jnp.full_like(m_sc, -jnp.inf)
    acc_sc[...] = jnp.zeros_like(acc_sc)
    score(0, s0_sc)

    def pair(pp, carry):
        j = 2 * pp
        score(j + 1, s1_sc)
        consume(j, s0_sc, False)
        score(j + 2, s0_sc)
        consume(j + 1, s1_sc, False)
        return carry

    lax.fori_loop(0, qi // 2, pair, 0)

    @pl.when(qi % 2 == 1)
    def _():
        score(qi, s1_sc)
        consume(qi - 1, s0_sc, False)
        consume(qi, s1_sc, True)

    @pl.when(qi % 2 == 0)
    def _():
        consume(qi, s0_sc, True)

    a0, a1 = acc_sc[0], acc_sc[1]
    o_ref[0] = jnp.where(lane < HEAD_DIM, a0 / pltpu.roll(a0, HEAD_DIM, axis=1),
                         a1 / pltpu.roll(a1, HEAD_DIM, axis=1))


def _fox_attention(qb, kb, vb, cum_bht, tq):
    b, t, d = qb.shape
    hp = N_HEADS // 2
    nt = t // tq
    cum5 = cum_bht.reshape(b, hp, 2, nt, tq).transpose(0, 1, 3, 2, 4)
    return pl.pallas_call(
        functools.partial(_fox_attn_kernel, tq=tq),
        grid=(b, hp, nt),
        in_specs=[pl.BlockSpec((1, tq, LANES), lambda i, h, q: (i, q, h)),
                  pl.BlockSpec((1, t, LANES), lambda i, h, q: (i, 0, h)),
                  pl.BlockSpec((1, t, LANES), lambda i, h, q: (i, 0, h)),
                  pl.BlockSpec((1, 1, nt, 2, tq), lambda i, h, q: (i, h, 0, 0, 0))],
        out_specs=pl.BlockSpec((1, tq, LANES), lambda i, h, q: (i, q, h)),
        out_shape=jax.ShapeDtypeStruct((b, t, d), F32),
        scratch_shapes=[pltpu.VMEM((2, tq, tq), F32), pltpu.VMEM((2, tq, tq), F32),
                        pltpu.VMEM((2, tq, LANES), F32), pltpu.VMEM((2, tq, LANES), F32)],
        compiler_params=_cparams("arbitrary", "arbitrary", "arbitrary"),
        name="fox_prompt_attention",
    )(qb, kb, vb, cum5)


def _split3(x):
    hi = x.astype(BF16)
    r1 = x - hi.astype(F32)
    mid = r1.astype(BF16)
    return hi, mid, (r1 - mid.astype(F32)).astype(BF16)


def _seg_mats():
    c = np.arange(LANES)
    same_head = (c[:, None] % N_HEADS) == (c[None, :] % N_HEADS)
    prefix = same_head & (c[:, None] <= c[None, :])
    strict = c[None, :] < c[:, None]
    return jnp.asarray(np.stack([prefix, same_head, strict]).astype(np.float32), dtype=BF16)


def _fox_decode_kernel(pt_ref, q_ref, *refs, n_q, n_par):
    del pt_ref
    k_refs, v_refs, lf_refs = refs[:n_par], refs[n_par:2 * n_par], refs[2 * n_par:3 * n_par]
    kn_ref, vn_ref, lfn_ref, seg_ref, o_ref, m_sc, l_sc, acc_sc, carry_sc = refs[3 * n_par:]
    p = pl.program_id(1)
    rows = n_q * N_HEADS

    @pl.when(p == 0)
    def _():
        m_sc[...] = jnp.full_like(m_sc, -jnp.inf)
        l_sc[...] = jnp.zeros_like(l_sc)
        acc_sc[...] = jnp.zeros_like(acc_sc)
        carry_sc[...] = jnp.zeros_like(carry_sc)

    q = q_ref[0]
    rr = lax.broadcasted_iota(jnp.int32, (rows, LANES), 0)
    cc = lax.broadcasted_iota(jnp.int32, (rows, LANES), 1)
    head_ok = (rr % N_HEADS) == (cc % N_HEADS)

    def cumulate(x):
        n = x.shape[0]
        tot = _dot3(x, seg_ref[1])
        strict = seg_ref[2][0:n, 0:n]
        excl = sum(jnp.dot(strict, piece, preferred_element_type=F32) for piece in _split3(tot))
        cum = carry_sc[...] + excl + _dot3(x, seg_ref[0])
        carry_sc[...] = carry_sc[...] + jnp.sum(tot, axis=0, keepdims=True)
        return cum * LOG2E

    def update(k2ds, v2ds, cums, ok):
        ss = []
        for k2d, cum in zip(k2ds, cums):
            s = lax.dot_general(q, k2d, (((1,), (1,)), ((), ())), preferred_element_type=F32)
            ss.append(jnp.concatenate(
                [jnp.where(ok, s[:, c * LANES:(c + 1) * LANES] - cum[c:c + 1, :], -jnp.inf)
                 for c in range(k2d.shape[0] // LANES)], axis=1))
        m = m_sc[...]
        m_new = m
        for s in ss:
            m_new = jnp.maximum(m_new, jnp.max(s, axis=-1, keepdims=True))
        alpha = jnp.exp2(m - m_new)
        l = alpha * l_sc[...]
        acc = alpha * acc_sc[...]
        for s, v2d in zip(ss, v2ds):
            pr = jnp.exp2(s - m_new)
            l = l + jnp.sum(pr, axis=-1, keepdims=True)
            acc = acc + jnp.dot(pr.astype(BF16), v2d, preferred_element_type=F32)
        l_sc[...] = l
        acc_sc[...] = acc
        m_sc[...] = m_new

    cols = PAGE_SIZE * N_HEADS
    update([r[0, 0].reshape(cols, HEAD_DIM).astype(BF16) for r in k_refs],
           [r[0, 0].reshape(cols, HEAD_DIM).astype(BF16) for r in v_refs],
           [cumulate(r[0, 0]) for r in lf_refs], head_ok)

    @pl.when(p == pl.num_programs(1) - 1)
    def _():
        causal = (cc // N_HEADS) <= (rr // N_HEADS)
        update([kn_ref[0]], [vn_ref[0]], [cumulate(lfn_ref[0])], head_ok & causal)
        o_ref[0] = acc_sc[...] / l_sc[...]


def _fox_decode(layer, page_table, q2d, cache_k, cache_v, lf_chunks, kn, vn, lfn, seg):
    b, rows, dh = q2d.shape
    n_q = rows // N_HEADS
    n_pages = page_table.shape[1]
    n_chunks = PAGE_SIZE * N_HEADS // LANES
    n_par = DECODE_PAGES_PER_STEP if n_pages % DECODE_PAGES_PER_STEP == 0 else 1
    page5 = lambda u: (lambda i, p, pt: (layer, pt[i, p * n_par + u], 0, 0, 0))
    page4 = lambda u: (lambda i, p, pt: (layer, pt[i, p * n_par + u], 0, 0))
    per_b = lambda i, p, pt: (i, 0, 0)
    kv_specs = [pl.BlockSpec((1, 1, PAGE_SIZE, N_HEADS, dh), page5(u)) for u in range(n_par)]
    lf_specs = [pl.BlockSpec((1, 1, n_chunks, LANES), page4(u)) for u in range(n_par)]
    grid_spec = pltpu.PrefetchScalarGridSpec(
        num_scalar_prefetch=1,
        grid=(b, n_pages // n_par),
        in_specs=[pl.BlockSpec((1, rows, dh), per_b)] + kv_specs + kv_specs + lf_specs + [
            pl.BlockSpec((1, LANES, dh), per_b),
            pl.BlockSpec((1, LANES, dh), per_b),
            pl.BlockSpec((1, SUBLANES, LANES), per_b),
            pl.BlockSpec((3, LANES, LANES), lambda i, p, pt: (0, 0, 0))],
        out_specs=pl.BlockSpec((1, rows, dh), per_b),
        scratch_shapes=[pltpu.VMEM((rows, 1), F32), pltpu.VMEM((rows, 1), F32),
                        pltpu.VMEM((rows, dh), F32), pltpu.VMEM((1, LANES), F32)])
    return pl.pallas_call(
        functools.partial(_fox_decode_kernel, n_q=n_q, n_par=n_par),
        grid_spec=grid_spec,
        out_shape=jax.ShapeDtypeStruct((b, rows, dh), F32),
        compiler_params=_cparams("arbitrary", "arbitrary"),
        name="fox_sample_attention",
    )(page_table, q2d, *([cache_k] * n_par), *([cache_v] * n_par), *([lf_chunks] * n_par),
      kn, vn, lfn, seg)


def _gated_out_kernel(o_ref, g_ref, res_ref, w_ref, y_ref):
    g = g_ref[...]
    z = o_ref[...] * (g * _sigmoid(g))
    y_ref[...] = res_ref[...] + _bdot(z, w_ref[...])


def _gated_out(o2d, g2d, res2d, w_out, tm):
    m, d = o2d.shape
    row = pl.BlockSpec((tm, d), lambda i: (i, 0))
    return pl.pallas_call(
        _gated_out_kernel,
        grid=(m // tm,),
        in_specs=[row, row, row, _const_spec((d, d))],
        out_specs=row,
        out_shape=jax.ShapeDtypeStruct((m, d), F32),
        compiler_params=_cparams("arbitrary"),
        name="gated_out_proj",
    )(o2d, g2d, res2d, w_out)


def _rwkv_pre_kernel(x_ref, xlast_ref, nw_ref, mu_ref, wr_ref, wk_ref, wg_ref, w1_ref, w2_ref,
                     a1_ref, a2_ref, vec_ref,
                     r_ref, dec_ref, k_ref, kk_ref, beta_ref, xv_ref, g_ref, last_ref,
                     carry_ref, *, tm, t_last):
    @pl.when(pl.program_id(1) == 0)
    def _():
        carry_ref[...] = xlast_ref[0]

    xn = _rmsnorm_rows(x_ref[0], nw_ref[...])
    rowi = lax.broadcasted_iota(jnp.int32, (tm, D_MODEL), 0)
    prev = jnp.where(rowi == 0, carry_ref[...], pltpu.roll(xn, 1, axis=0))
    carry_ref[...] = xn[tm - 1:tm, :]
    last_ref[0] = xn[t_last:t_last + 1, :]
    dx = prev - xn
    mix = lambda j: xn + dx * mu_ref[j:j + 1, :]

    w0, a0, k_k, k_a = (vec_ref[i:i + 1, :] for i in range(4))
    r = _bdot(mix(0), wr_ref[...])
    k = _bdot(mix(1), wk_ref[...])
    xv_ref[0] = mix(2).astype(BF16)
    g_ref[0] = _bdot(mix(3), wg_ref[...])
    wl = w0 + _bdot(jnp.tanh(_bdot(mix(4), w1_ref[...])), w2_ref[...])
    w_log = -(jnp.maximum(-wl, 0.0) + jnp.log(1.0 + jnp.exp(-jnp.abs(wl)))) - 0.5
    dec_ref[0] = jnp.exp(-jnp.exp(w_log))
    a = _sigmoid(a0 + _bdot(_bdot(mix(5), a1_ref[...]), a2_ref[...]))
    kk = k * k_k
    sq = kk * kk
    tot = sq[:, 0:LANES]
    for c in range(1, K_HI):
        tot = tot + sq[:, c * LANES:(c + 1) * LANES]
    inv = lax.rsqrt(jnp.maximum(_lane_group_sum(tot), L2_EPS_SQ))
    kkn = kk * jnp.concatenate([inv] * K_HI, axis=1)
    r_ref[0] = r
    k_ref[0] = k * (1.0 + (a - 1.0) * k_a)
    kk_ref[0] = kkn
    beta_ref[0] = kkn * a


def _rwkv_pre(x3, x_last, nw, mu8, wr, wk, wg, w1, w2, a1, a2, vecs, tm, t_last):
    b, t, d = x3.shape
    tile = pl.BlockSpec((1, tm, d), lambda i, j: (i, j, 0))
    per_b = pl.BlockSpec((1, 1, d), lambda i, j: (i, 0, 0))
    f32o = jax.ShapeDtypeStruct((b, t, d), F32)
    return pl.pallas_call(
        functools.partial(_rwkv_pre_kernel, tm=tm, t_last=t_last),
        grid=(b, t // tm),
        in_specs=[tile, per_b, _const_spec((1, d)), _const_spec((8, d)),
                  _const_spec((d, d)), _const_spec((d, d)), _const_spec((d, d)),
                  _const_spec((d, LANES)), _const_spec((LANES, d)),
                  _const_spec((d, LANES)), _const_spec((LANES, d)), _const_spec((8, d))],
        out_specs=[tile] * 7 + [per_b],
        out_shape=(f32o, f32o, f32o, f32o, f32o, jax.ShapeDtypeStruct((b, t, d), BF16), f32o,
                   jax.ShapeDtypeStruct((b, 1, d), F32)),
        scratch_shapes=[pltpu.VMEM((1, d), F32)],
        compiler_params=_cparams("arbitrary", "arbitrary"),
        name="rwkv_pre",
    )(x3, x_last, nw, mu8, wr, wk, wg, w1, w2, a1, a2, vecs)


def _matmul_kernel(a_ref, b_ref, o_ref):
    o_ref[...] = jnp.dot(a_ref[...], b_ref[...], preferred_element_type=F32)


def _matmul_bf16(a, bmat, tm, tn):
    m, k = a.shape
    n = bmat.shape[1]
    return pl.pallas_call(
        _matmul_kernel,
        grid=(m // tm, n // tn),
        in_specs=[pl.BlockSpec((tm, k), lambda i, j: (i, 0)),
                  pl.BlockSpec((k, tn), lambda i, j: (0, j))],
        out_specs=pl.BlockSpec((tm, tn), lambda i, j: (i, j)),
        out_shape=jax.ShapeDtypeStruct((m, n), F32),
        compiler_params=_cparams("arbitrary", "arbitrary"),
        name="matmul_bf16",
    )(a, bmat)


def _wkv_kernel(r_ref, dec_ref, k_ref, kk_ref, beta_ref, vc_ref, s0_ref, lnw_ref, lnb_ref, rk_ref,
                z_ref, sf_ref, *scs, nb, tb):
    s_scs, sa_scs, ya_scs = scs[:nb], scs[nb:2 * nb], scs[2 * nb:]

    @pl.when(pl.program_id(1) == 0)
    def _():
        for b in range(nb):
            s_scs[b][...] = s0_ref[b]

    def rows_total(x):
        return _lane_group_sum(jnp.sum(x, axis=0, keepdims=True))

    def tree_sum(xs):
        while len(xs) > 1:
            xs = [xs[i] + xs[i + 1] for i in range(0, len(xs), 2)]
        return xs[0]

    def reduce_sa(b, t):
        return -_lane_group_sum(
            tree_sum([s_scs[b][c] * kk_ref[b, t, c:c + 1, :] for c in range(K_HI)]))

    def reduce_y(b, t):
        wr = dec_ref[b, t] * r_ref[b, t]
        ya_scs[b][...] = tree_sum([s_scs[b][c] * wr[c:c + 1, :] for c in range(K_HI)])

    def update(b, t, sa):
        w = dec_ref[b, t]
        km = k_ref[b, t]
        beta = beta_ref[b, t]
        vc = vc_ref[b, t]
        for c in range(K_HI):
            s_scs[b][c] = (s_scs[b][c] * w[c:c + 1, :] + sa * beta[c:c + 1, :]
                           + vc * km[c:c + 1, :])
        sa_scs[b][...] = sa

    def tail(b, t):
        r = r_ref[b, t]
        km = k_ref[b, t]
        vc = vc_ref[b, t]
        y = (_lane_group_sum(ya_scs[b][...]) + sa_scs[b][...] * rows_total(beta_ref[b, t] * r)
             + vc * rows_total(km * r))
        mean = jnp.sum(y, axis=0, keepdims=True) * (1.0 / HEAD_DIM)
        dy = y - mean
        var = jnp.sum(dy * dy, axis=0, keepdims=True) * (1.0 / HEAD_DIM)
        yn = dy * lax.rsqrt(var + GN_EPS)
        bonus = rows_total(r * km * rk_ref[...])
        z_ref[b, t] = yn * lnw_ref[...] + lnb_ref[...] + vc * bonus

    def step(t, carry):
        sas = [reduce_sa(b, t) for b in range(nb)]
        for b in range(nb):
            tail(b, t - 1)
        for b in range(nb):
            reduce_y(b, t)
        for b in range(nb):
            update(b, t, sas[b])
        return carry

    sas = [reduce_sa(b, 0) for b in range(nb)]
    for b in range(nb):
        reduce_y(b, 0)
    for b in range(nb):
        update(b, 0, sas[b])
    lax.fori_loop(1, tb, step, 0)
    for b in range(nb):
        tail(b, tb - 1)

    @pl.when(pl.program_id(1) == pl.num_programs(1) - 1)
    def _():
        for b in range(nb):
            sf_ref[b] = s_scs[b][...]


def _wkv_scan(r4, dec4, k4, kk4, beta4, vc4, s0, lnw, lnb, rk, nb, tb):
    b, t = r4.shape[:2]
    rowf = pl.BlockSpec((nb, tb, K_HI, LANES), lambda i, j: (i, j, 0, 0))
    colf = pl.BlockSpec((nb, tb, HEAD_DIM, LANES), lambda i, j: (i, j, 0, 0))
    st = pl.BlockSpec((nb, K_HI, HEAD_DIM, LANES), lambda i, j: (i, 0, 0, 0))
    return pl.pallas_call(
        functools.partial(_wkv_kernel, nb=nb, tb=tb),
        grid=(b // nb, t // tb),
        in_specs=[rowf] * 5 + [colf, st, _const_spec((HEAD_DIM, LANES)),
                               _const_spec((HEAD_DIM, LANES)), _const_spec((K_HI, LANES))],
        out_specs=[colf, st],
        out_shape=(jax.ShapeDtypeStruct((b, t, HEAD_DIM, LANES), F32),
                   jax.ShapeDtypeStruct((b, K_HI, HEAD_DIM, LANES), F32)),
        scratch_shapes=([pltpu.VMEM((K_HI, HEAD_DIM, LANES), F32) for _ in range(nb)]
                        + [pltpu.VMEM((HEAD_DIM, LANES), F32) for _ in range(2 * nb)]),
        compiler_params=_cparams("arbitrary", "arbitrary"),
        name="wkv_scan",
    )(r4, dec4, k4, kk4, beta4, vc4, s0, lnw, lnb, rk)


def _pool_kernel(x_ref, hist_ref, nw_ref, win_ref, wgrp_ref, scale_ref, wout_ref,
                 y_ref, tail_ref, ext_ref, *, tm, t_valid, pos0):
    j = pl.program_id(1)

    @pl.when(j == 0)
    def _():
        ext_ref[0:POOL_HIST, :] = hist_ref[0]

    x = x_ref[0]
    xb = _rmsnorm_rows(x, nw_ref[...]).astype(BF16)
    u = jnp.dot(xb, win_ref[:, 0:D_MODEL], preferred_element_type=F32)
    g = jnp.dot(xb, win_ref[:, D_MODEL:2 * D_MODEL], preferred_element_type=F32)
    ext_ref[POOL_HIST:POOL_HIST + tm, :] = u
    pos = pos0 + j * tm + lax.broadcasted_iota(jnp.int32, (tm, POOL_GROUP), 0)
    mixed = []
    for gi, w in enumerate(POOL_WINDOWS):
        lo, hi = gi * POOL_GROUP, (gi + 1) * POOL_GROUP
        win = u[:, lo:hi]
        for dlt in range(1, w):
            win = win + ext_ref[POOL_HIST - dlt:POOL_HIST - dlt + tm, lo:hi]
        cnt = jnp.minimum(pos + 1, w).astype(F32)
        diff = win / cnt - u[:, lo:hi]
        mixed.append(_bdot(diff, wgrp_ref[gi]))
    mix = jnp.concatenate(mixed, axis=1) * scale_ref[...]
    z = mix * (g * _sigmoid(g))
    y_ref[0] = x + _bdot(z, wout_ref[...])
    tail = ext_ref[t_valid:t_valid + POOL_HIST, :]
    tail_ref[0] = tail
    ext_ref[0:POOL_HIST, :] = tail


def _pool_mixer(x3, hist, nw, w_in, w_grp, scale, w_out, tm, t_valid, pos0):
    b, t, d = x3.shape
    tile = pl.BlockSpec((1, tm, d), lambda i, j: (i, j, 0))
    per_b = pl.BlockSpec((1, POOL_HIST, d), lambda i, j: (i, 0, 0))
    return pl.pallas_call(
        functools.partial(_pool_kernel, tm=tm, t_valid=t_valid, pos0=pos0),
        grid=(b, t // tm),
        in_specs=[tile, per_b, _const_spec((1, d)), _const_spec((d, 2 * d)),
                  _const_spec((len(POOL_WINDOWS), POOL_GROUP, POOL_GROUP)),
                  _const_spec((1, d)), _const_spec((d, d))],
        out_specs=[tile, per_b],
        out_shape=(jax.ShapeDtypeStruct((b, t, d), F32),
                   jax.ShapeDtypeStruct((b, POOL_HIST, d), F32)),
        scratch_shapes=[pltpu.VMEM((POOL_HIST + tm, d), F32)],
        compiler_params=_cparams("arbitrary", "arbitrary"),
        name="pool_mixer",
    )(x3, hist, nw, w_in, w_grp, scale, w_out)


def _key_perm():
    n = np.arange(D_MODEL)
    kc, kl, h = n // LANES, (n % LANES) // N_HEADS, n % N_HEADS
    return h * HEAD_DIM + kc * SUBLANES + kl


def _value_rep_perm():
    n = np.arange(HEAD_DIM * LANES)
    return (n % N_HEADS) * HEAD_DIM + n // LANES


def _pad_rows(x3, rows):
    b, t, d = x3.shape
    return jnp.concatenate([x3, jnp.zeros((b, rows - t, d), x3.dtype)], axis=1)


def _state_to_tiles(s):
    b = s.shape[0]
    s = s.reshape(b, N_HEADS, HEAD_DIM, K_HI, SUBLANES)
    return s.transpose(0, 3, 2, 4, 1).reshape(b, K_HI, HEAD_DIM, LANES)


def _tiles_to_state(s):
    b = s.shape[0]
    s = s.reshape(b, K_HI, HEAD_DIM, SUBLANES, N_HEADS)
    return s.transpose(0, 4, 2, 1, 3).reshape(b, N_HEADS, HEAD_DIM, HEAD_DIM)


def _head_col_tile(vec):
    return jnp.tile(vec.reshape(N_HEADS, HEAD_DIM).T, (1, SUBLANES))


def _fox_layer(j, y_p, y_s, nw, cache_k, cache_v, lf_chunks, page_table, fox_w_in, fox_b_f,
               fox_q_gain, fox_k_gain, fox_w_out, tri, seg):
    d, h = D_MODEL, N_HEADS
    bp, tp, _ = y_p.shape
    bs, ts, _ = y_s.shape
    w_in = fox_w_in[j]
    w_main = w_in[:, :4 * d].astype(BF16)
    w_f = jnp.pad(w_in[:, 4 * d:], ((0, 0), (0, LANES - h))).astype(BF16)
    b_f = jnp.pad(fox_b_f[j], (0, LANES - h)).reshape(1, LANES)
    qg = jnp.tile(fox_q_gain[j], h).reshape(1, d)
    kg = jnp.tile(fox_k_gain[j], h).reshape(1, d)
    gmat = jnp.asarray(np.kron(np.eye(h), np.ones((HEAD_DIM, HEAD_DIM))), dtype=BF16)
    w_out = fox_w_out[j].astype(BF16)

    xp = y_p.reshape(bp * tp, d)
    qb, k, kb, v, vb, g, lf = _fox_proj(xp, nw, w_main, w_f, b_f, qg, kg, gmat, tm=256)
    lf_p = lf[:, :h].reshape(bp, tp, h)
    cum = _cumsum_time(lf_p.transpose(0, 2, 1), tri)
    sh = (bp, tp, d)
    o = _fox_attention(qb.reshape(sh), kb.reshape(sh), vb.reshape(sh), cum, tq=min(tp, 512))
    y_p_new = _gated_out(o.reshape(bp * tp, d), g, xp, w_out, tm=512).reshape(bp, tp, d)
    new_p = (k.reshape(bp, tp, h, HEAD_DIM), v.reshape(bp, tp, h, HEAD_DIM), lf_p)

    xs = y_s.reshape(bs * ts, d)
    qb, k, kb, v, vb, g, lf = _fox_proj(xs, nw, w_main, w_f, b_f, qg, kg, gmat, tm=bs * ts)
    lf_s = lf[:, :h].reshape(bs, ts, h)
    q2d = qb.reshape(bs, ts * h, HEAD_DIM)
    kn = _pad_rows(kb.reshape(bs, ts * h, HEAD_DIM), LANES)
    vn = _pad_rows(vb.reshape(bs, ts * h, HEAD_DIM), LANES)
    lfn = jnp.pad(lf_s.reshape(bs, 1, ts * h), ((0, 0), (0, SUBLANES - 1), (0, LANES - ts * h)))
    o = _fox_decode(j, page_table, q2d, cache_k, cache_v, lf_chunks, kn, vn, lfn, seg)
    y_s_new = _gated_out(o.reshape(bs * ts, d), g, xs, w_out, tm=bs * ts).reshape(bs, ts, d)
    new_s = (k.reshape(bs, ts, h, HEAD_DIM), v.reshape(bs, ts, h, HEAD_DIM), lf_s)
    return y_p_new, y_s_new, new_p, new_s


def _rwkv_group(y3, x_last, s0_tiles, nw, prm, tm, t_valid, nb, tb):
    b, t, d = y3.shape
    (mu8, wr, wk, wv_rep, wg, w1, w2, a1, a2, vecs, rk, lnw, lnb, w_out) = prm
    r, dec, k, kk, beta, xv, g, last = _rwkv_pre(y3, x_last, nw, mu8, wr, wk, wg, w1, w2, a1, a2,
                                                 vecs, tm=tm, t_last=(t_valid - 1) % tm)
    m = b * t
    vc = _matmul_bf16(xv.reshape(m, d), wv_rep, tm=min(m, 512), tn=2048)
    rowf = lambda z: z[:, :t_valid].reshape(b, t_valid, K_HI, LANES)
    vc4 = vc.reshape(b, t, HEAD_DIM, LANES)[:, :t_valid]
    z, s_fin = _wkv_scan(rowf(r), rowf(dec), rowf(k), rowf(kk), rowf(beta), vc4, s0_tiles,
                         lnw, lnb, rk, nb=nb, tb=tb)
    z = z[..., :N_HEADS].transpose(0, 1, 3, 2).reshape(b, t_valid, d)
    if t_valid != t:
        z = _pad_rows(z, t)
    tmo = min(m, 512)
    y_new = _gated_out(z.reshape(m, d), g.reshape(m, d), y3.reshape(m, d), w_out, tm=tmo)
    return y_new.reshape(b, t, d), s_fin, last[:, 0]


def _rwkv_params(j, rwkv_mu, rwkv_w_rkvg, rwkv_w0, rwkv_w1, rwkv_w2, rwkv_a0, rwkv_a1, rwkv_a2,
                 rwkv_k_k, rwkv_k_a, rwkv_r_k, rwkv_ln_w, rwkv_ln_b, rwkv_w_out):
    d = D_MODEL
    perm = _key_perm()
    lora = rwkv_w1.shape[-1]
    mu8 = jnp.pad(rwkv_mu[j], ((0, 2), (0, 0)))
    wr = rwkv_w_rkvg[j, 0][:, perm].astype(BF16)
    wk = rwkv_w_rkvg[j, 1][:, perm].astype(BF16)
    wv_rep = rwkv_w_rkvg[j, 2][:, _value_rep_perm()].astype(BF16)
    wg = rwkv_w_rkvg[j, 3].astype(BF16)
    w1 = jnp.pad(rwkv_w1[j], ((0, 0), (0, LANES - lora))).astype(BF16)
    w2 = jnp.pad(rwkv_w2[j][:, perm], ((0, LANES - lora), (0, 0))).astype(BF16)
    a1 = jnp.pad(rwkv_a1[j], ((0, 0), (0, LANES - lora))).astype(BF16)
    a2 = jnp.pad(rwkv_a2[j][:, perm], ((0, LANES - lora), (0, 0))).astype(BF16)
    vecs = jnp.stack([rwkv_w0[j][perm], rwkv_a0[j][perm], rwkv_k_k[j][perm], rwkv_k_a[j][perm]])
    vecs = jnp.pad(vecs, ((0, 4), (0, 0)))
    rk = rwkv_r_k[j].reshape(d)[perm].reshape(K_HI, LANES)
    lnw = _head_col_tile(rwkv_ln_w[j])
    lnb = _head_col_tile(rwkv_ln_b[j])
    return (mu8, wr, wk, wv_rep, wg, w1, w2, a1, a2, vecs, rk, lnw, lnb, rwkv_w_out[j].astype(BF16))


def kernel(x_prompt, x_sample, cache_k, cache_v, cache_logf, page_table, state_wkv, state_shift,
           state_pool, norm_w, fox_w_in, fox_b_f, fox_q_gain, fox_k_gain, fox_w_out,
           rwkv_mu, rwkv_w_rkvg, rwkv_w0, rwkv_w1, rwkv_w2, rwkv_a0, rwkv_a1, rwkv_a2,
           rwkv_k_k, rwkv_k_a, rwkv_r_k, rwkv_ln_w, rwkv_ln_b, rwkv_w_out,
           pool_w_in, pool_w_grp, pool_scale, pool_w_out):
    d = D_MODEL
    depth = norm_w.shape[0]
    bp, tp, _ = x_prompt.shape
    bs, ts, _ = x_sample.shape
    past = page_table.shape[1] * PAGE_SIZE
    n_fox, n_pool_pages = cache_k.shape[:2]
    lf_chunks = cache_logf.reshape(n_fox, n_pool_pages, PAGE_SIZE * N_HEADS // LANES, LANES)
    tri = _tri_mats()
    seg = _seg_mats()
    ts_pad = SUBLANES * pl.cdiv(ts, SUBLANES)

    y_p, y_s = x_prompt, x_sample
    kp, vp, lp, ks, vs, ls = [], [], [], [], [], []
    wkv_p, sh_p, wkv_s, sh_s, pool_p, pool_s = [], [], [], [], [], []
    for i in range(depth):
        kind, j = i % 3, i // 3
        nw = norm_w[i].reshape(1, d)
        if kind == 0:
            y_p, y_s, new_p, new_s = _fox_layer(
                j, y_p, y_s, nw, cache_k, cache_v, lf_chunks, page_table, fox_w_in, fox_b_f,
                fox_q_gain, fox_k_gain, fox_w_out, tri, seg)
            kp.append(new_p[0]); vp.append(new_p[1]); lp.append(new_p[2])
            ks.append(new_s[0]); vs.append(new_s[1]); ls.append(new_s[2])
        elif kind == 1:
            prm = _rwkv_params(j, rwkv_mu, rwkv_w_rkvg, rwkv_w0, rwkv_w1, rwkv_w2, rwkv_a0,
                               rwkv_a1, rwkv_a2, rwkv_k_k, rwkv_k_a, rwkv_r_k, rwkv_ln_w,
                               rwkv_ln_b, rwkv_w_out)
            zero_state = jnp.zeros((bp, K_HI, HEAD_DIM, LANES), F32)
            y_p, s_fin, last = _rwkv_group(y_p, jnp.zeros((bp, 1, d), F32), zero_state, nw, prm,
                                           tm=min(tp, 256), t_valid=tp, nb=bp, tb=min(tp, 64))
            wkv_p.append(_tiles_to_state(s_fin)); sh_p.append(last)
            y_s_pad, s_fin, last = _rwkv_group(
                _pad_rows(y_s, ts_pad), state_shift[j].reshape(bs, 1, d),
                _state_to_tiles(state_wkv[j]), nw, prm, tm=ts_pad, t_valid=ts, nb=2, tb=ts)
            y_s = y_s_pad[:, :ts]
            wkv_s.append(_tiles_to_state(s_fin)); sh_s.append(last)
        else:
            w_in = pool_w_in[j].astype(BF16)
            w_grp = pool_w_grp[j].astype(BF16)
            scale = pool_scale[j].reshape(1, d)
            w_out = pool_w_out[j].astype(BF16)
            y_p, tail = _pool_mixer(y_p, jnp.zeros((bp, POOL_HIST, d), F32), nw, w_in, w_grp,
                                    scale, w_out, tm=min(tp, 256), t_valid=min(tp, 256), pos0=0)
            pool_p.append(tail[:, 1:])
            hist = jnp.pad(state_pool[j], ((0, 0), (1, 0), (0, 0)))
            y_s_pad, tail = _pool_mixer(_pad_rows(y_s, ts_pad), hist, nw, w_in, w_grp, scale,
                                        w_out, tm=ts_pad, t_valid=ts, pos0=past)
            y_s = y_s_pad[:, :ts]
            pool_s.append(tail[:, 1:])
    st = lambda xs: jnp.stack(xs, 0)
    return (y_p, y_s, st(kp), st(vp), st(lp), st(ks), st(vs), st(ls),
            st(wkv_p), st(sh_p), st(wkv_s), st(sh_s), st(pool_p), st(pool_s))
```

```python
import functools

import jax
import jax.numpy as jnp
import numpy as np
from jax import lax
from jax.experimental import pallas as pl
from jax.experimental.pallas import tpu as pltpu

F32 = jnp.float32
BF16 = jnp.bfloat16

D_MODEL = 1024
HEAD_DIM = 64
N_HEADS = D_MODEL // HEAD_DIM
PAGE_SIZE = 128
POOL_WINDOWS = (2, 4, 8, 16)
POOL_GROUP = D_MODEL // len(POOL_WINDOWS)
POOL_HIST = 16
RMS_EPS = 1e-6
GN_EPS = 64e-5
L2_EPS_SQ = 1e-24
ATTN_SCALE = HEAD_DIM ** -0.5
LOG2E = 1.4426950408889634

LANES = 128
SUBLANES = 8
K_HI = HEAD_DIM // SUBLANES
VMEM_LIMIT = 56 * 1024 * 1024
DECODE_PAGES_PER_STEP = 8


def _cparams(*sem):
    return pltpu.CompilerParams(dimension_semantics=sem, vmem_limit_bytes=VMEM_LIMIT)


def _const_spec(shape):
    nd = len(shape)
    return pl.BlockSpec(shape, lambda *_: (0,) * nd, pipeline_mode=pl.Buffered(1))


def _rmsnorm_rows(x, w):
    ms = jnp.mean(x * x, axis=-1, keepdims=True)
    return x * lax.rsqrt(ms + RMS_EPS) * w


def _sigmoid(x):
    return 1.0 / (1.0 + jnp.exp(-x))


def _bdot(a, b):
    return jnp.dot(a.astype(BF16), b, preferred_element_type=F32)


def _dot3(x, m):
    hi = x.astype(BF16)
    r1 = x - hi.astype(F32)
    mid = r1.astype(BF16)
    lo = (r1 - mid.astype(F32)).astype(BF16)
    return (jnp.dot(hi, m, preferred_element_type=F32)
            + jnp.dot(mid, m, preferred_element_type=F32)
            + jnp.dot(lo, m, preferred_element_type=F32))


def _lane_group_sum(x):
    x = x + pltpu.roll(x, 16, axis=x.ndim - 1)
    x = x + pltpu.roll(x, 32, axis=x.ndim - 1)
    return x + pltpu.roll(x, 64, axis=x.ndim - 1)


def _fox_proj_kernel(x_ref, nw_ref, w_ref, wf_ref, bf_ref, qg_ref, kg_ref, gm_ref,
                     qb_ref, k_ref, kb_ref, v_ref, vb_ref, g_ref, lf_ref):
    xb = _rmsnorm_rows(x_ref[...], nw_ref[...]).astype(BF16)

    def headnorm(z, gain):
        ms = _bdot(z * z, gm_ref[...]) * (1.0 / HEAD_DIM)
        return z * lax.rsqrt(ms + RMS_EPS) * gain

    d = D_MODEL
    q = headnorm(jnp.dot(xb, w_ref[:, 0:d], preferred_element_type=F32), qg_ref[...])
    qb_ref[...] = (q * (ATTN_SCALE * LOG2E)).astype(BF16)
    k = headnorm(jnp.dot(xb, w_ref[:, d:2 * d], preferred_element_type=F32), kg_ref[...])
    k_ref[...] = k
    kb_ref[...] = k.astype(BF16)
    v = jnp.dot(xb, w_ref[:, 2 * d:3 * d], preferred_element_type=F32)
    v_ref[...] = v
    vb_ref[...] = v.astype(BF16)
    g_ref[...] = jnp.dot(xb, w_ref[:, 3 * d:4 * d], preferred_element_type=F32)
    fl = jnp.dot(xb, wf_ref[...], preferred_element_type=F32) + bf_ref[...]
    lf_ref[...] = -(jnp.maximum(-fl, 0.0) + jnp.log(1.0 + jnp.exp(-jnp.abs(fl))))


def _fox_proj(x2d, nw, w_main, w_f, b_f, qg, kg, gmat, tm):
    m = x2d.shape[0]
    d = D_MODEL
    row = lambda n: pl.BlockSpec((tm, n), lambda i: (i, 0))
    outs = (jax.ShapeDtypeStruct((m, d), BF16), jax.ShapeDtypeStruct((m, d), F32),
            jax.ShapeDtypeStruct((m, d), BF16), jax.ShapeDtypeStruct((m, d), F32),
            jax.ShapeDtypeStruct((m, d), BF16), jax.ShapeDtypeStruct((m, d), F32),
            jax.ShapeDtypeStruct((m, LANES), F32))
    return pl.pallas_call(
        _fox_proj_kernel,
        grid=(m // tm,),
        in_specs=[row(d), _const_spec((1, d)), _const_spec((d, 4 * d)), _const_spec((d, LANES)),
                  _const_spec((1, LANES)), _const_spec((1, d)), _const_spec((1, d)),
                  _const_spec((d, d))],
        out_specs=[row(d)] * 6 + [row(LANES)],
        out_shape=outs,
        compiler_params=_cparams("arbitrary"),
        name="fox_proj",
    )(x2d, nw, w_main, w_f, b_f, qg, kg, gmat)


def _cumsum_kernel(x_ref, tri_ref, o_ref, carry_ref, *, nchunk):
    @pl.when(pl.program_id(1) == 0)
    def _():
        carry_ref[...] = jnp.zeros_like(carry_ref)

    carry = carry_ref[...]
    upper = tri_ref[0]
    ones = tri_ref[1]
    for c in range(nchunk):
        x = x_ref[0, :, c * LANES:(c + 1) * LANES]
        o_ref[0, :, c * LANES:(c + 1) * LANES] = (carry + _dot3(x, upper)) * LOG2E
        carry = carry + _dot3(x, ones)
    carry_ref[...] = carry


def _tri_mats():
    i = np.arange(LANES)
    upper = (i[:, None] <= i[None, :]).astype(np.float32)
    return jnp.asarray(np.stack([upper, np.ones_like(upper)]), dtype=BF16)


def _cumsum_time(x_bht, tri):
    b, h, t = x_bht.shape
    tc = min(t, 1024)
    return pl.pallas_call(
        functools.partial(_cumsum_kernel, nchunk=tc // LANES),
        grid=(b, t // tc),
        in_specs=[pl.BlockSpec((1, h, tc), lambda i, j: (i, 0, j)), _const_spec((2, LANES, LANES))],
        out_specs=pl.BlockSpec((1, h, tc), lambda i, j: (i, 0, j)),
        out_shape=jax.ShapeDtypeStruct((b, h, t), F32),
        scratch_shapes=[pltpu.VMEM((h, LANES), F32)],
        compiler_params=_cparams("arbitrary", "arbitrary"),
        name="logf_cumsum",
    )(x_bht, tri)


def _fox_attn_kernel(q_ref, k_ref, v_ref, c_ref, o_ref, s0_sc, s1_sc, m_sc, acc_sc, *, tq):
    qi = pl.program_id(2)
    q2 = q_ref[0]
    lane = lax.broadcasted_iota(jnp.int32, (tq, LANES), 1)
    zero = jnp.zeros_like(q2)
    qh = (jnp.where(lane < HEAD_DIM, q2, zero), jnp.where(lane >= HEAD_DIM, q2, zero))

    def score(kj, dst):
        k2 = k_ref[0, pl.ds(pl.multiple_of(kj * tq, tq), tq), :]
        for hh in range(2):
            dst[hh] = lax.dot_general(qh[hh], k2, (((1,), (1,)), ((), ())),
                                      preferred_element_type=F32)

    def consume(kj, src, diagonal):
        v2 = v_ref[0, pl.ds(pl.multiple_of(kj * tq, tq), tq), :]
        one = jnp.ones_like(v2)
        vh = (jnp.where(lane < HEAD_DIM, v2, one), jnp.where(lane >= HEAD_DIM, v2, one))
        cj = c_ref[0, 0, kj]
        for hh in range(2):
            s = src[hh] - cj[hh:hh + 1, :]
            if diagonal:
                row = lax.broadcasted_iota(jnp.int32, (tq, tq), 0)
                col = lax.broadcasted_iota(jnp.int32, (tq, tq), 1)
                s = jnp.where(row >= col, s, -jnp.inf)
            part = s[:, 0:LANES]
            for c in range(1, tq // LANES):
                part = jnp.maximum(part, s[:, c * LANES:(c + 1) * LANES])
            m = m_sc[hh]
            m_new = jnp.maximum(m, jnp.max(part, axis=-1, keepdims=True))
            p = jnp.exp2(s - jnp.concatenate([m_new] * (tq // LANES), axis=1))
            acc_sc[hh] = jnp.exp2(m - m_new) * acc_sc[hh] + jnp.dot(
                p.astype(BF16), vh[hh], preferred_element_type=F32)
            m_sc[hh] = m_new

    m_sc[...] = jnp.full_like(m_sc, -jnp.inf)
    acc_sc[...] = jnp.zeros_like(acc_sc)
    score(0, s0_sc)

    def pair(pp, carry):
        j = 2 * pp
        score(j + 1, s1_sc)
        consume(j, s0_sc, False)
        score(j + 2, s0_sc)
        consume(j + 1, s1_sc, False)
        return carry

    lax.fori_loop(0, qi // 2, pair, 0)

    @pl.when(qi % 2 == 1)
    def _():
        score(qi, s1_sc)
        consume(qi - 1, s0_sc, False)
        consume(qi, s1_sc, True)

    @pl.when(qi % 2 == 0)
    def _():
        consume(qi, s0_sc, True)

    a0, a1 = acc_sc[0], acc_sc[1]
    o_ref[0] = jnp.where(lane < HEAD_DIM, a0 / pltpu.roll(a0, HEAD_DIM, axis=1),
                         a1 / pltpu.roll(a1, HEAD_DIM, axis=1))


def _fox_attention(qb, kb, vb, cum_bht, tq):
    b, t, d = qb.shape
    hp = N_HEADS // 2
    nt = t // tq
    cum5 = cum_bht.reshape(b, hp, 2, nt, tq).transpose(0, 1, 3, 2, 4)
    return pl.pallas_call(
        functools.partial(_fox_attn_kernel, tq=tq),
        grid=(b, hp, nt),
        in_specs=[pl.BlockSpec((1, tq, LANES), lambda i, h, q: (i, q, h)),
                  pl.BlockSpec((1, t, LANES), lambda i, h, q: (i, 0, h)),
                  pl.BlockSpec((1, t, LANES), lambda i, h, q: (i, 0, h)),
                  pl.BlockSpec((1, 1, nt, 2, tq), lambda i, h, q: (i, h, 0, 0, 0))],
        out_specs=pl.BlockSpec((1, tq, LANES), lambda i, h, q: (i, q, h)),
        out_shape=jax.ShapeDtypeStruct((b, t, d), F32),
        scratch_shapes=[pltpu.VMEM((2, tq, tq), F32), pltpu.VMEM((2, tq, tq), F32),
                        pltpu.VMEM((2, tq, LANES), F32), pltpu.VMEM((2, tq, LANES), F32)],
        compiler_params=_cparams("arbitrary", "arbitrary", "arbitrary"),
        name="fox_prompt_attention",
    )(qb, kb, vb, cum5)


def _fox_decode_kernel(pt_ref, q_ref, *refs, n_q, n_par):
    del pt_ref
    k_refs, v_refs, lf_refs = refs[:n_par], refs[n_par:2 * n_par], refs[2 * n_par:3 * n_par]
    kn_ref, vn_ref, lfn_ref, tri_ref, o_ref, m_sc, l_sc, acc_sc, carry_sc = refs[3 * n_par:]
    p = pl.program_id(1)
    rows = n_q * N_HEADS

    @pl.when(p == 0)
    def _():
        m_sc[...] = jnp.full_like(m_sc, -jnp.inf)
        l_sc[...] = jnp.zeros_like(l_sc)
        acc_sc[...] = jnp.zeros_like(acc_sc)
        carry_sc[...] = jnp.zeros_like(carry_sc)

    q = q_ref[0]

    def cumulate(lft):
        cum = carry_sc[...] + _dot3(lft, tri_ref[0])
        carry_sc[...] = carry_sc[...] + _dot3(lft, tri_ref[1])
        return cum * LOG2E

    def update(kts, vts, cums, ok):
        ss = []
        for kt, cum in zip(kts, cums):
            s = jnp.dot(q, kt, preferred_element_type=F32) - jnp.concatenate([cum] * n_q, axis=0)
            ss.append(s if ok is None else jnp.where(ok, s, -jnp.inf))
        m = m_sc[...]
        m_new = m
        for s in ss:
            m_new = jnp.maximum(m_new, jnp.max(s, axis=-1, keepdims=True))
        alpha = jnp.exp2(m - m_new)
        l = alpha * l_sc[...]
        acc = alpha * acc_sc[...]
        for s, vt in zip(ss, vts):
            pr = jnp.exp2(s - m_new)
            l = l + jnp.sum(pr, axis=-1, keepdims=True)
            acc = acc + lax.dot_general(pr.astype(BF16), vt, (((1,), (1,)), ((), ())),
                                        preferred_element_type=F32)
        l_sc[...] = l
        acc_sc[...] = acc
        m_sc[...] = m_new

    update([r[0, 0].reshape(D_MODEL, PAGE_SIZE).astype(BF16) for r in k_refs],
           [r[0, 0].reshape(D_MODEL, PAGE_SIZE).astype(BF16) for r in v_refs],
           [cumulate(r[0, 0]) for r in lf_refs], None)

    @pl.when(p == pl.num_programs(1) - 1)
    def _():
        r = lax.broadcasted_iota(jnp.int32, (rows, PAGE_SIZE), 0)
        j = lax.broadcasted_iota(jnp.int32, (rows, PAGE_SIZE), 1)
        update([kn_ref[0]], [vn_ref[0]], [cumulate(lfn_ref[0])], j <= r // N_HEADS)
        o = acc_sc[...] / l_sc[...]
        rr = lax.broadcasted_iota(jnp.int32, (rows, D_MODEL), 0)
        cc = lax.broadcasted_iota(jnp.int32, (rows, D_MODEL), 1)
        o = jnp.where(rr % N_HEADS == cc // HEAD_DIM, o, 0.0)
        o_ref[0] = jnp.concatenate(
            [jnp.sum(o[N_HEADS * t:N_HEADS * (t + 1)], axis=0, keepdims=True) for t in range(n_q)],
            axis=0)


def _fox_decode(layer, page_table, q_bd, cache_kt, cache_vt, cache_lft, knt, vnt, lfn, tri):
    b, rows, d = q_bd.shape
    n_q = rows // N_HEADS
    n_pages = page_table.shape[1]
    n_par = DECODE_PAGES_PER_STEP if n_pages % DECODE_PAGES_PER_STEP == 0 else 1
    page5 = lambda u: (lambda i, p, pt: (layer, pt[i, p * n_par + u], 0, 0, 0))
    page4 = lambda u: (lambda i, p, pt: (layer, pt[i, p * n_par + u], 0, 0))
    per_b = lambda i, p, pt: (i, 0, 0)
    kv_specs = [pl.BlockSpec((1, 1, N_HEADS, HEAD_DIM, PAGE_SIZE), page5(u)) for u in range(n_par)]
    lf_specs = [pl.BlockSpec((1, 1, N_HEADS, PAGE_SIZE), page4(u)) for u in range(n_par)]
    grid_spec = pltpu.PrefetchScalarGridSpec(
        num_scalar_prefetch=1,
        grid=(b, n_pages // n_par),
        in_specs=[pl.BlockSpec((1, rows, d), per_b)] + kv_specs + kv_specs + lf_specs + [
            pl.BlockSpec((1, d, PAGE_SIZE), per_b),
            pl.BlockSpec((1, d, PAGE_SIZE), per_b),
            pl.BlockSpec((1, N_HEADS, PAGE_SIZE), per_b),
            pl.BlockSpec((2, LANES, LANES), lambda i, p, pt: (0, 0, 0))],
        out_specs=pl.BlockSpec((1, n_q, d), per_b),
        scratch_shapes=[pltpu.VMEM((rows, 1), F32), pltpu.VMEM((rows, 1), F32),
                        pltpu.VMEM((rows, d), F32), pltpu.VMEM((N_HEADS, LANES), F32)])
    return pl.pallas_call(
        functools.partial(_fox_decode_kernel, n_q=n_q, n_par=n_par),
        grid_spec=grid_spec,
        out_shape=jax.ShapeDtypeStruct((b, n_q, d), F32),
        compiler_params=_cparams("arbitrary", "arbitrary"),
        name="fox_sample_attention",
    )(page_table, q_bd, *([cache_kt] * n_par), *([cache_vt] * n_par), *([cache_lft] * n_par),
      knt, vnt, lfn, tri)


def _gated_out_kernel(o_ref, g_ref, res_ref, w_ref, y_ref):
    g = g_ref[...]
    z = o_ref[...] * (g * _sigmoid(g))
    y_ref[...] = res_ref[...] + _bdot(z, w_ref[...])


def _gated_out(o2d, g2d, res2d, w_out, tm):
    m, d = o2d.shape
    row = pl.BlockSpec((tm, d), lambda i: (i, 0))
    return pl.pallas_call(
        _gated_out_kernel,
        grid=(m // tm,),
        in_specs=[row, row, row, _const_spec((d, d))],
        out_specs=row,
        out_shape=jax.ShapeDtypeStruct((m, d), F32),
        compiler_params=_cparams("arbitrary"),
        name="gated_out_proj",
    )(o2d, g2d, res2d, w_out)


def _rwkv_pre_kernel(x_ref, xlast_ref, nw_ref, mu_ref, wr_ref, wk_ref, wg_ref, w1_ref, w2_ref,
                     a1_ref, a2_ref, vec_ref,
                     r_ref, dec_ref, k_ref, kk_ref, beta_ref, xv_ref, g_ref, last_ref,
                     carry_ref, *, tm, t_last):
    @pl.when(pl.program_id(1) == 0)
    def _():
        carry_ref[...] = xlast_ref[0]

    xn = _rmsnorm_rows(x_ref[0], nw_ref[...])
    rowi = lax.broadcasted_iota(jnp.int32, (tm, D_MODEL), 0)
    prev = jnp.where(rowi == 0, carry_ref[...], pltpu.roll(xn, 1, axis=0))
    carry_ref[...] = xn[tm - 1:tm, :]
    last_ref[0] = xn[t_last:t_last + 1, :]
    dx = prev - xn
    mix = lambda j: xn + dx * mu_ref[j:j + 1, :]

    w0, a0, k_k, k_a = (vec_ref[i:i + 1, :] for i in range(4))
    r = _bdot(mix(0), wr_ref[...])
    k = _bdot(mix(1), wk_ref[...])
    xv_ref[0] = mix(2).astype(BF16)
    g_ref[0] = _bdot(mix(3), wg_ref[...])
    wl = w0 + _bdot(jnp.tanh(_bdot(mix(4), w1_ref[...])), w2_ref[...])
    w_log = -(jnp.maximum(-wl, 0.0) + jnp.log(1.0 + jnp.exp(-jnp.abs(wl)))) - 0.5
    dec_ref[0] = jnp.exp(-jnp.exp(w_log))
    a = _sigmoid(a0 + _bdot(_bdot(mix(5), a1_ref[...]), a2_ref[...]))
    kk = k * k_k
    sq = kk * kk
    tot = sq[:, 0:LANES]
    for c in range(1, K_HI):
        tot = tot + sq[:, c * LANES:(c + 1) * LANES]
    inv = lax.rsqrt(jnp.maximum(_lane_group_sum(tot), L2_EPS_SQ))
    kkn = kk * jnp.concatenate([inv] * K_HI, axis=1)
    r_ref[0] = r
    k_ref[0] = k * (1.0 + (a - 1.0) * k_a)
    kk_ref[0] = kkn
    beta_ref[0] = kkn * a


def _rwkv_pre(x3, x_last, nw, mu8, wr, wk, wg, w1, w2, a1, a2, vecs, tm, t_last):
    b, t, d = x3.shape
    tile = pl.BlockSpec((1, tm, d), lambda i, j: (i, j, 0))
    per_b = pl.BlockSpec((1, 1, d), lambda i, j: (i, 0, 0))
    f32o = jax.ShapeDtypeStruct((b, t, d), F32)
    return pl.pallas_call(
        functools.partial(_rwkv_pre_kernel, tm=tm, t_last=t_last),
        grid=(b, t // tm),
        in_specs=[tile, per_b, _const_spec((1, d)), _const_spec((8, d)),
                  _const_spec((d, d)), _const_spec((d, d)), _const_spec((d, d)),
                  _const_spec((d, LANES)), _const_spec((LANES, d)),
                  _const_spec((d, LANES)), _const_spec((LANES, d)), _const_spec((8, d))],
        out_specs=[tile] * 7 + [per_b],
        out_shape=(f32o, f32o, f32o, f32o, f32o, jax.ShapeDtypeStruct((b, t, d), BF16), f32o,
                   jax.ShapeDtypeStruct((b, 1, d), F32)),
        scratch_shapes=[pltpu.VMEM((1, d), F32)],
        compiler_params=_cparams("arbitrary", "arbitrary"),
        name="rwkv_pre",
    )(x3, x_last, nw, mu8, wr, wk, wg, w1, w2, a1, a2, vecs)


def _matmul_kernel(a_ref, b_ref, o_ref):
    o_ref[...] = jnp.dot(a_ref[...], b_ref[...], preferred_element_type=F32)


def _matmul_bf16(a, bmat, tm, tn):
    m, k = a.shape
    n = bmat.shape[1]
    return pl.pallas_call(
        _matmul_kernel,
        grid=(m // tm, n // tn),
        in_specs=[pl.BlockSpec((tm, k), lambda i, j: (i, 0)),
                  pl.BlockSpec((k, tn), lambda i, j: (0, j))],
        out_specs=pl.BlockSpec((tm, tn), lambda i, j: (i, j)),
        out_shape=jax.ShapeDtypeStruct((m, n), F32),
        compiler_params=_cparams("arbitrary", "arbitrary"),
        name="matmul_bf16",
    )(a, bmat)


def _wkv_kernel(r_ref, dec_ref, k_ref, kk_ref, beta_ref, vc_ref, s0_ref, lnw_ref, lnb_ref, rk_ref,
                z_ref, sf_ref, *scs, nb, tb):
    s_scs, sa_scs, ya_scs = scs[:nb], scs[nb:2 * nb], scs[2 * nb:]

    @pl.when(pl.program_id(1) == 0)
    def _():
        for b in range(nb):
            s_scs[b][...] = s0_ref[b]

    def rows_total(x):
        return _lane_group_sum(jnp.sum(x, axis=0, keepdims=True))

    def tree_sum(xs):
        while len(xs) > 1:
            xs = [xs[i] + xs[i + 1] for i in range(0, len(xs), 2)]
        return xs[0]

    def reduce_sa(b, t):
        return -_lane_group_sum(
            tree_sum([s_scs[b][c] * kk_ref[b, t, c:c + 1, :] for c in range(K_HI)]))

    def reduce_y(b, t):
        wr = dec_ref[b, t] * r_ref[b, t]
        ya_scs[b][...] = tree_sum([s_scs[b][c] * wr[c:c + 1, :] for c in range(K_HI)])

    def update(b, t, sa):
        w = dec_ref[b, t]
        km = k_ref[b, t]
        beta = beta_ref[b, t]
        vc = vc_ref[b, t]
        for c in range(K_HI):
            s_scs[b][c] = (s_scs[b][c] * w[c:c + 1, :] + sa * beta[c:c + 1, :]
                           + vc * km[c:c + 1, :])
        sa_scs[b][...] = sa

    def tail(b, t):
        r = r_ref[b, t]
        km = k_ref[b, t]
        vc = vc_ref[b, t]
        y = (_lane_group_sum(ya_scs[b][...]) + sa_scs[b][...] * rows_total(beta_ref[b, t] * r)
             + vc * rows_total(km * r))
        mean = jnp.sum(y, axis=0, keepdims=True) * (1.0 / HEAD_DIM)
        dy = y - mean
        var = jnp.sum(dy * dy, axis=0, keepdims=True) * (1.0 / HEAD_DIM)
        yn = dy * lax.rsqrt(var + GN_EPS)
        bonus = rows_total(r * km * rk_ref[...])
        z_ref[b, t] = yn * lnw_ref[...] + lnb_ref[...] + vc * bonus

    def step(t, carry):
        sas = [reduce_sa(b, t) for b in range(nb)]
        for b in range(nb):
            tail(b, t - 1)
        for b in range(nb):
            reduce_y(b, t)
        for b in range(nb):
            update(b, t, sas[b])
        return carry

    sas = [reduce_sa(b, 0) for b in range(nb)]
    for b in range(nb):
        reduce_y(b, 0)
    for b in range(nb):
        update(b, 0, sas[b])
    lax.fori_loop(1, tb, step, 0)
    for b in range(nb):
        tail(b, tb - 1)

    @pl.when(pl.program_id(1) == pl.num_programs(1) - 1)
    def _():
        for b in range(nb):
            sf_ref[b] = s_scs[b][...]


def _wkv_scan(r4, dec4, k4, kk4, beta4, vc4, s0, lnw, lnb, rk, nb, tb):
    b, t = r4.shape[:2]
    rowf = pl.BlockSpec((nb, tb, K_HI, LANES), lambda i, j: (i, j, 0, 0))
    colf = pl.BlockSpec((nb, tb, HEAD_DIM, LANES), lambda i, j: (i, j, 0, 0))
    st = pl.BlockSpec((nb, K_HI, HEAD_DIM, LANES), lambda i, j: (i, 0, 0, 0))
    return pl.pallas_call(
        functools.partial(_wkv_kernel, nb=nb, tb=tb),
        grid=(b // nb, t // tb),
        in_specs=[rowf] * 5 + [colf, st, _const_spec((HEAD_DIM, LANES)),
                               _const_spec((HEAD_DIM, LANES)), _const_spec((K_HI, LANES))],
        out_specs=[colf, st],
        out_shape=(jax.ShapeDtypeStruct((b, t, HEAD_DIM, LANES), F32),
                   jax.ShapeDtypeStruct((b, K_HI, HEAD_DIM, LANES), F32)),
        scratch_shapes=([pltpu.VMEM((K_HI, HEAD_DIM, LANES), F32) for _ in range(nb)]
                        + [pltpu.VMEM((HEAD_DIM, LANES), F32) for _ in range(2 * nb)]),
        compiler_params=_cparams("arbitrary", "arbitrary"),
        name="wkv_scan",
    )(r4, dec4, k4, kk4, beta4, vc4, s0, lnw, lnb, rk)


def _pool_kernel(x_ref, hist_ref, nw_ref, win_ref, wgrp_ref, scale_ref, wout_ref,
                 y_ref, tail_ref, ext_ref, *, tm, t_valid, pos0):
    j = pl.program_id(1)

    @pl.when(j == 0)
    def _():
        ext_ref[0:POOL_HIST, :] = hist_ref[0]

    x = x_ref[0]
    xb = _rmsnorm_rows(x, nw_ref[...]).astype(BF16)
    u = jnp.dot(xb, win_ref[:, 0:D_MODEL], preferred_element_type=F32)
    g = jnp.dot(xb, win_ref[:, D_MODEL:2 * D_MODEL], preferred_element_type=F32)
    ext_ref[POOL_HIST:POOL_HIST + tm, :] = u
    pos = pos0 + j * tm + lax.broadcasted_iota(jnp.int32, (tm, POOL_GROUP), 0)
    mixed = []
    for gi, w in enumerate(POOL_WINDOWS):
        lo, hi = gi * POOL_GROUP, (gi + 1) * POOL_GROUP
        win = u[:, lo:hi]
        for dlt in range(1, w):
            win = win + ext_ref[POOL_HIST - dlt:POOL_HIST - dlt + tm, lo:hi]
        cnt = jnp.minimum(pos + 1, w).astype(F32)
        diff = win / cnt - u[:, lo:hi]
        mixed.append(_bdot(diff, wgrp_ref[gi]))
    mix = jnp.concatenate(mixed, axis=1) * scale_ref[...]
    z = mix * (g * _sigmoid(g))
    y_ref[0] = x + _bdot(z, wout_ref[...])
    tail = ext_ref[t_valid:t_valid + POOL_HIST, :]
    tail_ref[0] = tail
    ext_ref[0:POOL_HIST, :] = tail


def _pool_mixer(x3, hist, nw, w_in, w_grp, scale, w_out, tm, t_valid, pos0):
    b, t, d = x3.shape
    tile = pl.BlockSpec((1, tm, d), lambda i, j: (i, j, 0))
    per_b = pl.BlockSpec((1, POOL_HIST, d), lambda i, j: (i, 0, 0))
    return pl.pallas_call(
        functools.partial(_pool_kernel, tm=tm, t_valid=t_valid, pos0=pos0),
        grid=(b, t // tm),
        in_specs=[tile, per_b, _const_spec((1, d)), _const_spec((d, 2 * d)),
                  _const_spec((len(POOL_WINDOWS), POOL_GROUP, POOL_GROUP)),
                  _const_spec((1, d)), _const_spec((d, d))],
        out_specs=[tile, per_b],
        out_shape=(jax.ShapeDtypeStruct((b, t, d), F32),
                   jax.ShapeDtypeStruct((b, POOL_HIST, d), F32)),
        scratch_shapes=[pltpu.VMEM((POOL_HIST + tm, d), F32)],
        compiler_params=_cparams("arbitrary", "arbitrary"),
        name="pool_mixer",
    )(x3, hist, nw, w_in, w_grp, scale, w_out)


def _key_perm():
    n = np.arange(D_MODEL)
    kc, kl, h = n // LANES, (n % LANES) // N_HEADS, n % N_HEADS
    return h * HEAD_DIM + kc * SUBLANES + kl


def _value_rep_perm():
    n = np.arange(HEAD_DIM * LANES)
    return (n % N_HEADS) * HEAD_DIM + n // LANES


def _pad_rows(x3, rows):
    b, t, d = x3.shape
    return jnp.concatenate([x3, jnp.zeros((b, rows - t, d), x3.dtype)], axis=1)


def _state_to_tiles(s):
    b = s.shape[0]
    s = s.reshape(b, N_HEADS, HEAD_DIM, K_HI, SUBLANES)
    return s.transpose(0, 3, 2, 4, 1).reshape(b, K_HI, HEAD_DIM, LANES)


def _tiles_to_state(s):
    b = s.shape[0]
    s = s.reshape(b, K_HI, HEAD_DIM, SUBLANES, N_HEADS)
    return s.transpose(0, 4, 2, 1, 3).reshape(b, N_HEADS, HEAD_DIM, HEAD_DIM)


def _head_col_tile(vec):
    return jnp.tile(vec.reshape(N_HEADS, HEAD_DIM).T, (1, SUBLANES))


def _fox_layer(j, y_p, y_s, nw, cache_kt, cache_vt, cache_lft, page_table, fox_w_in, fox_b_f,
               fox_q_gain, fox_k_gain, fox_w_out, tri):
    d, h = D_MODEL, N_HEADS
    bp, tp, _ = y_p.shape
    bs, ts, _ = y_s.shape
    w_in = fox_w_in[j]
    w_main = w_in[:, :4 * d].astype(BF16)
    w_f = jnp.pad(w_in[:, 4 * d:], ((0, 0), (0, LANES - h))).astype(BF16)
    b_f = jnp.pad(fox_b_f[j], (0, LANES - h)).reshape(1, LANES)
    qg = jnp.tile(fox_q_gain[j], h).reshape(1, d)
    kg = jnp.tile(fox_k_gain[j], h).reshape(1, d)
    gmat = jnp.asarray(np.kron(np.eye(h), np.ones((HEAD_DIM, HEAD_DIM))), dtype=BF16)
    w_out = fox_w_out[j].astype(BF16)

    xp = y_p.reshape(bp * tp, d)
    qb, k, kb, v, vb, g, lf = _fox_proj(xp, nw, w_main, w_f, b_f, qg, kg, gmat, tm=256)
    lf_p = lf[:, :h].reshape(bp, tp, h)
    cum = _cumsum_time(lf_p.transpose(0, 2, 1), tri)
    sh = (bp, tp, d)
    o = _fox_attention(qb.reshape(sh), kb.reshape(sh), vb.reshape(sh), cum, tq=min(tp, 512))
    y_p_new = _gated_out(o.reshape(bp * tp, d), g, xp, w_out, tm=512).reshape(bp, tp, d)
    new_p = (k.reshape(bp, tp, h, HEAD_DIM), v.reshape(bp, tp, h, HEAD_DIM), lf_p)

    xs = y_s.reshape(bs * ts, d)
    qb, k, kb, v, vb, g, lf = _fox_proj(xs, nw, w_main, w_f, b_f, qg, kg, gmat, tm=bs * ts)
    lf_s = lf[:, :h].reshape(bs, ts, h)
    head_of_lane = np.arange(d) // HEAD_DIM
    head_mask = jnp.asarray(head_of_lane[None, :] == np.arange(h)[:, None])
    q_bd = jnp.where(head_mask[None, None], qb.reshape(bs, ts, 1, d), jnp.zeros((), BF16))
    q_bd = q_bd.reshape(bs, ts * h, d)
    as_page = lambda z: jnp.pad(z.reshape(bs, ts, d).transpose(0, 2, 1),
                                ((0, 0), (0, 0), (0, PAGE_SIZE - ts)))
    lfn = jnp.pad(lf_s.transpose(0, 2, 1), ((0, 0), (0, 0), (0, PAGE_SIZE - ts)))
    o = _fox_decode(j, page_table, q_bd, cache_kt, cache_vt, cache_lft, as_page(kb), as_page(vb),
                    lfn, tri)
    y_s_new = _gated_out(o.reshape(bs * ts, d), g, xs, w_out, tm=bs * ts).reshape(bs, ts, d)
    new_s = (k.reshape(bs, ts, h, HEAD_DIM), v.reshape(bs, ts, h, HEAD_DIM), lf_s)
    return y_p_new, y_s_new, new_p, new_s


def _rwkv_group(y3, x_last, s0_tiles, nw, prm, tm, t_valid, nb, tb):
    b, t, d = y3.shape
    (mu8, wr, wk, wv_rep, wg, w1, w2, a1, a2, vecs, rk, lnw, lnb, w_out) = prm
    r, dec, k, kk, beta, xv, g, last = _rwkv_pre(y3, x_last, nw, mu8, wr, wk, wg, w1, w2, a1, a2,
                                                 vecs, tm=tm, t_last=(t_valid - 1) % tm)
    m = b * t
    vc = _matmul_bf16(xv.reshape(m, d), wv_rep, tm=min(m, 512), tn=2048)
    rowf = lambda z: z[:, :t_valid].reshape(b, t_valid, K_HI, LANES)
    vc4 = vc.reshape(b, t, HEAD_DIM, LANES)[:, :t_valid]
    z, s_fin = _wkv_scan(rowf(r), rowf(dec), rowf(k), rowf(kk), rowf(beta), vc4, s0_tiles,
                         lnw, lnb, rk, nb=nb, tb=tb)
    z = z[..., :N_HEADS].transpose(0, 1, 3, 2).reshape(b, t_valid, d)
    if t_valid != t:
        z = _pad_rows(z, t)
    tmo = min(m, 512)
    y_new = _gated_out(z.reshape(m, d), g.reshape(m, d), y3.reshape(m, d), w_out, tm=tmo)
    return y_new.reshape(b, t, d), s_fin, last[:, 0]


def _rwkv_params(j, rwkv_mu, rwkv_w_rkvg, rwkv_w0, rwkv_w1, rwkv_w2, rwkv_a0, rwkv_a1, rwkv_a2,
                 rwkv_k_k, rwkv_k_a, rwkv_r_k, rwkv_ln_w, rwkv_ln_b, rwkv_w_out):
    d = D_MODEL
    perm = _key_perm()
    lora = rwkv_w1.shape[-1]
    mu8 = jnp.pad(rwkv_mu[j], ((0, 2), (0, 0)))
    wr = rwkv_w_rkvg[j, 0][:, perm].astype(BF16)
    wk = rwkv_w_rkvg[j, 1][:, perm].astype(BF16)
    wv_rep = rwkv_w_rkvg[j, 2][:, _value_rep_perm()].astype(BF16)
    wg = rwkv_w_rkvg[j, 3].astype(BF16)
    w1 = jnp.pad(rwkv_w1[j], ((0, 0), (0, LANES - lora))).astype(BF16)
    w2 = jnp.pad(rwkv_w2[j][:, perm], ((0, LANES - lora), (0, 0))).astype(BF16)
    a1 = jnp.pad(rwkv_a1[j], ((0, 0), (0, LANES - lora))).astype(BF16)
    a2 = jnp.pad(rwkv_a2[j][:, perm], ((0, LANES - lora), (0, 0))).astype(BF16)
    vecs = jnp.stack([rwkv_w0[j][perm], rwkv_a0[j][perm], rwkv_k_k[j][perm], rwkv_k_a[j][perm]])
    vecs = jnp.pad(vecs, ((0, 4), (0, 0)))
    rk = rwkv_r_k[j].reshape(d)[perm].reshape(K_HI, LANES)
    lnw = _head_col_tile(rwkv_ln_w[j])
    lnb = _head_col_tile(rwkv_ln_b[j])
    return (mu8, wr, wk, wv_rep, wg, w1, w2, a1, a2, vecs, rk, lnw, lnb, rwkv_w_out[j].astype(BF16))


def kernel(x_prompt, x_sample, cache_k, cache_v, cache_logf, page_table, state_wkv, state_shift,
           state_pool, norm_w, fox_w_in, fox_b_f, fox_q_gain, fox_k_gain, fox_w_out,
           rwkv_mu, rwkv_w_rkvg, rwkv_w0, rwkv_w1, rwkv_w2, rwkv_a0, rwkv_a1, rwkv_a2,
           rwkv_k_k, rwkv_k_a, rwkv_r_k, rwkv_ln_w, rwkv_ln_b, rwkv_w_out,
           pool_w_in, pool_w_grp, pool_scale, pool_w_out):
    d = D_MODEL
    depth = norm_w.shape[0]
    bp, tp, _ = x_prompt.shape
    bs, ts, _ = x_sample.shape
    past = page_table.shape[1] * PAGE_SIZE
    n_fox, n_pool_pages = cache_k.shape[:2]
    cache_kt = cache_k.transpose(0, 1, 3, 4, 2)
    cache_vt = cache_v.transpose(0, 1, 3, 4, 2)
    cache_lft = cache_logf.transpose(0, 1, 3, 2)
    tri = _tri_mats()
    ts_pad = SUBLANES * pl.cdiv(ts, SUBLANES)

    y_p, y_s = x_prompt, x_sample
    kp, vp, lp, ks, vs, ls = [], [], [], [], [], []
    wkv_p, sh_p, wkv_s, sh_s, pool_p, pool_s = [], [], [], [], [], []
    for i in range(depth):
        kind, j = i % 3, i // 3
        nw = norm_w[i].reshape(1, d)
        if kind == 0:
            y_p, y_s, new_p, new_s = _fox_layer(
                j, y_p, y_s, nw, cache_kt, cache_vt, cache_lft, page_table, fox_w_in, fox_b_f,
                fox_q_gain, fox_k_gain, fox_w_out, tri)
            kp.append(new_p[0]); vp.append(new_p[1]); lp.append(new_p[2])
            ks.append(new_s[0]); vs.append(new_s[1]); ls.append(new_s[2])
        elif kind == 1:
            prm = _rwkv_params(j, rwkv_mu, rwkv_w_rkvg, rwkv_w0, rwkv_w1, rwkv_w2, rwkv_a0,
                               rwkv_a1, rwkv_a2, rwkv_k_k, rwkv_k_a, rwkv_r_k, rwkv_ln_w,
                               rwkv_ln_b, rwkv_w_out)
            zero_state = jnp.zeros((bp, K_HI, HEAD_DIM, LANES), F32)
            y_p, s_fin, last = _rwkv_group(y_p, jnp.zeros((bp, 1, d), F32), zero_state, nw, prm,
                                           tm=min(tp, 256), t_valid=tp, nb=bp, tb=min(tp, 64))
            wkv_p.append(_tiles_to_state(s_fin)); sh_p.append(last)
            y_s_pad, s_fin, last = _rwkv_group(
                _pad_rows(y_s, ts_pad), state_shift[j].reshape(bs, 1, d),
                _state_to_tiles(state_wkv[j]), nw, prm, tm=ts_pad, t_valid=ts, nb=2, tb=ts)
            y_s = y_s_pad[:, :ts]
            wkv_s.append(_tiles_to_state(s_fin)); sh_s.append(last)
        else:
            w_in = pool_w_in[j].astype(BF16)
            w_grp = pool_w_grp[j].astype(BF16)
            scale = pool_scale[j].reshape(1, d)
            w_out = pool_w_out[j].astype(BF16)
            y_p, tail = _pool_mixer(y_p, jnp.zeros((bp, POOL_HIST, d), F32), nw, w_in, w_grp,
                                    scale, w_out, tm=min(tp, 256), t_valid=min(tp, 256), pos0=0)
            pool_p.append(tail[:, 1:])
            hist = jnp.pad(state_pool[j], ((0, 0), (1, 0), (0, 0)))
            y_s_pad, tail = _pool_mixer(_pad_rows(y_s, ts_pad), hist, nw, w_in, w_grp, scale,
                                        w_out, tm=ts_pad, t_valid=ts, pos0=past)
            y_s = y_s_pad[:, :ts]
            pool_s.append(tail[:, 1:])
    st = lambda xs: jnp.stack(xs, 0)
    return (y_p, y_s, st(kp), st(vp), st(lp), st(ks), st(vs), st(ls),
            st(wkv_p), st(sh_p), st(wkv_s), st(sh_s), st(pool_p), st(pool_s))
```

```python
import functools

import jax
import jax.numpy as jnp
import numpy as np
from jax import lax
from jax.experimental import pallas as pl
from jax.experimental.pallas import tpu as pltpu

F32 = jnp.float32
BF16 = jnp.bfloat16

D_MODEL = 1024
HEAD_DIM = 64
N_HEADS = D_MODEL // HEAD_DIM
PAGE_SIZE = 128
POOL_WINDOWS = (2, 4, 8, 16)
POOL_GROUP = D_MODEL // len(POOL_WINDOWS)
POOL_HIST = 16
RMS_EPS = 1e-6
GN_EPS = 64e-5
L2_EPS_SQ = 1e-24
ATTN_SCALE = HEAD_DIM ** -0.5
LOG2E = 1.4426950408889634

LANES = 128
SUBLANES = 8
K_HI = HEAD_DIM // SUBLANES
VMEM_LIMIT = 56 * 1024 * 1024
DECODE_PAGES_PER_STEP = 8


def _cparams(*sem):
    return pltpu.CompilerParams(dimension_semantics=sem, vmem_limit_bytes=VMEM_LIMIT)


def _const_spec(shape):
    nd = len(shape)
    return pl.BlockSpec(shape, lambda *_: (0,) * nd, pipeline_mode=pl.Buffered(1))


def _rmsnorm_rows(x, w):
    ms = jnp.mean(x * x, axis=-1, keepdims=True)
    return x * lax.rsqrt(ms + RMS_EPS) * w


def _sigmoid(x):
    return 1.0 / (1.0 + jnp.exp(-x))


def _bdot(a, b):
    return jnp.dot(a.astype(BF16), b, preferred_element_type=F32)


def _dot3(x, m):
    hi = x.astype(BF16)
    r1 = x - hi.astype(F32)
    mid = r1.astype(BF16)
    lo = (r1 - mid.astype(F32)).astype(BF16)
    return (jnp.dot(hi, m, preferred_element_type=F32)
            + jnp.dot(mid, m, preferred_element_type=F32)
            + jnp.dot(lo, m, preferred_element_type=F32))


def _lane_group_sum(x):
    x = x + pltpu.roll(x, 16, axis=x.ndim - 1)
    x = x + pltpu.roll(x, 32, axis=x.ndim - 1)
    return x + pltpu.roll(x, 64, axis=x.ndim - 1)


def _fox_proj_kernel(x_ref, nw_ref, w_ref, wf_ref, bf_ref, qg_ref, kg_ref, gm_ref, *refs,
                     time_minor):
    if time_minor:
        qb_ref, kb_ref, vb_ref, g_ref, kt_ref, vt_ref, lft_ref = refs[-7:]
    else:
        qb_ref, k_ref, kb_ref, v_ref, vb_ref, g_ref, lf_ref = refs
    xb = _rmsnorm_rows(x_ref[...], nw_ref[...]).astype(BF16)

    def headnorm(z, gain):
        ms = _bdot(z * z, gm_ref[...]) * (1.0 / HEAD_DIM)
        return z * lax.rsqrt(ms + RMS_EPS) * gain

    d = D_MODEL
    q = headnorm(jnp.dot(xb, w_ref[:, 0:d], preferred_element_type=F32), qg_ref[...])
    qb_ref[...] = (q * (ATTN_SCALE * LOG2E)).astype(BF16)
    k = headnorm(jnp.dot(xb, w_ref[:, d:2 * d], preferred_element_type=F32), kg_ref[...])
    kb_ref[...] = k.astype(BF16)
    v = jnp.dot(xb, w_ref[:, 2 * d:3 * d], preferred_element_type=F32)
    vb_ref[...] = v.astype(BF16)
    g_ref[...] = jnp.dot(xb, w_ref[:, 3 * d:4 * d], preferred_element_type=F32)
    fl = jnp.dot(xb, wf_ref[...], preferred_element_type=F32) + bf_ref[...]
    lf = -(jnp.maximum(-fl, 0.0) + jnp.log(1.0 + jnp.exp(-jnp.abs(fl))))
    if time_minor:
        tm = k.shape[0]
        kt_ref[0, 0] = k.T.reshape(N_HEADS, HEAD_DIM, tm)
        vt_ref[0, 0] = v.T.reshape(N_HEADS, HEAD_DIM, tm)
        lft_ref[0, 0] = lf.T[0:N_HEADS, :]
    else:
        k_ref[...] = k
        v_ref[...] = v
        lf_ref[...] = lf


def _fox_proj(x2d, nw, w_main, w_f, b_f, qg, kg, gmat, tm):
    m = x2d.shape[0]
    d = D_MODEL
    row = lambda n: pl.BlockSpec((tm, n), lambda i: (i, 0))
    outs = (jax.ShapeDtypeStruct((m, d), BF16), jax.ShapeDtypeStruct((m, d), F32),
            jax.ShapeDtypeStruct((m, d), BF16), jax.ShapeDtypeStruct((m, d), F32),
            jax.ShapeDtypeStruct((m, d), BF16), jax.ShapeDtypeStruct((m, d), F32),
            jax.ShapeDtypeStruct((m, LANES), F32))
    return pl.pallas_call(
        functools.partial(_fox_proj_kernel, time_minor=False),
        grid=(m // tm,),
        in_specs=[row(d), _const_spec((1, d)), _const_spec((d, 4 * d)), _const_spec((d, LANES)),
                  _const_spec((1, LANES)), _const_spec((1, d)), _const_spec((1, d)),
                  _const_spec((d, d))],
        out_specs=[row(d)] * 6 + [row(LANES)],
        out_shape=outs,
        compiler_params=_cparams("arbitrary"),
        name="fox_proj",
    )(x2d, nw, w_main, w_f, b_f, qg, kg, gmat)


def _fox_proj_prompt(layer, n_layers, x3, nw, w_main, w_f, b_f, qg, kg, gmat, stacked, tm):
    b, t, d = x3.shape
    nt = t // tm
    m = b * t
    row = pl.BlockSpec((tm, d), lambda i: (i, 0))
    kv_spec = pl.BlockSpec((1, 1, N_HEADS, HEAD_DIM, tm), lambda i: (layer, i // nt, 0, 0, i % nt))
    lf_spec = pl.BlockSpec((1, 1, N_HEADS, tm), lambda i: (layer, i // nt, 0, i % nt))
    kv_shape = jax.ShapeDtypeStruct((n_layers, b, N_HEADS, HEAD_DIM, t), F32)
    outs = (jax.ShapeDtypeStruct((m, d), BF16), jax.ShapeDtypeStruct((m, d), BF16),
            jax.ShapeDtypeStruct((m, d), BF16), jax.ShapeDtypeStruct((m, d), F32),
            kv_shape, kv_shape, jax.ShapeDtypeStruct((n_layers, b, N_HEADS, t), F32))
    in_specs = [row, _const_spec((1, d)), _const_spec((d, 4 * d)), _const_spec((d, LANES)),
                _const_spec((1, LANES)), _const_spec((1, d)), _const_spec((1, d)),
                _const_spec((d, d))]
    args = [x3.reshape(m, d), nw, w_main, w_f, b_f, qg, kg, gmat]
    aliases = {}
    if stacked is not None:
        in_specs += [pl.BlockSpec(memory_space=pl.ANY)] * 3
        aliases = {len(args) + i: 4 + i for i in range(3)}
        args += list(stacked)
    return pl.pallas_call(
        functools.partial(_fox_proj_kernel, time_minor=True),
        grid=(m // tm,),
        in_specs=in_specs,
        out_specs=[row] * 4 + [kv_spec, kv_spec, lf_spec],
        out_shape=outs,
        input_output_aliases=aliases,
        compiler_params=_cparams("arbitrary"),
        name="fox_proj_prompt",
    )(*args)


def _cumsum_kernel(x_ref, tri_ref, o_ref, carry_ref, *, nchunk):
    @pl.when(pl.program_id(1) == 0)
    def _():
        carry_ref[...] = jnp.zeros_like(carry_ref)

    carry = carry_ref[...]
    upper = tri_ref[0]
    ones = tri_ref[1]
    for c in range(nchunk):
        x = x_ref[0, :, c * LANES:(c + 1) * LANES]
        o_ref[0, :, c * LANES:(c + 1) * LANES] = (carry + _dot3(x, upper)) * LOG2E
        carry = carry + _dot3(x, ones)
    carry_ref[...] = carry


def _tri_mats():
    i = np.arange(LANES)
    upper = (i[:, None] <= i[None, :]).astype(np.float32)
    return jnp.asarray(np.stack([upper, np.ones_like(upper)]), dtype=BF16)


def _cumsum_time(x_bht, tri):
    b, h, t = x_bht.shape
    tc = min(t, 1024)
    return pl.pallas_call(
        functools.partial(_cumsum_kernel, nchunk=tc // LANES),
        grid=(b, t // tc),
        in_specs=[pl.BlockSpec((1, h, tc), lambda i, j: (i, 0, j)), _const_spec((2, LANES, LANES))],
        out_specs=pl.BlockSpec((1, h, tc), lambda i, j: (i, 0, j)),
        out_shape=jax.ShapeDtypeStruct((b, h, t), F32),
        scratch_shapes=[pltpu.VMEM((h, LANES), F32)],
        compiler_params=_cparams("arbitrary", "arbitrary"),
        name="logf_cumsum",
    )(x_bht, tri)


def _fox_attn_kernel(q_ref, k_ref, v_ref, c_ref, o_ref, s0_sc, s1_sc, m_sc, acc_sc, *, tq):
    qi = pl.program_id(2)
    q2 = q_ref[0]
    lane = lax.broadcasted_iota(jnp.int32, (tq, LANES), 1)
    zero = jnp.zeros_like(q2)
    qh = (jnp.where(lane < HEAD_DIM, q2, zero), jnp.where(lane >= HEAD_DIM, q2, zero))

    def score(kj, dst):
        k2 = k_ref[0, pl.ds(pl.multiple_of(kj * tq, tq), tq), :]
        for hh in range(2):
            dst[hh] = lax.dot_general(qh[hh], k2, (((1,), (1,)), ((), ())),
                                      preferred_element_type=F32)

    def consume(kj, src, diagonal):
        v2 = v_ref[0, pl.ds(pl.multiple_of(kj * tq, tq), tq), :]
        one = jnp.ones_like(v2)
        vh = (jnp.where(lane < HEAD_DIM, v2, one), jnp.where(lane >= HEAD_DIM, v2, one))
        cj = c_ref[0, 0, kj]
        for hh in range(2):
            s = src[hh] - cj[hh:hh + 1, :]
            if diagonal:
                row = lax.broadcasted_iota(jnp.int32, (tq, tq), 0)
                col = lax.broadcasted_iota(jnp.int32, (tq, tq), 1)
                s = jnp.where(row >= col, s, -jnp.inf)
            part = s[:, 0:LANES]
            for c in range(1, tq // LANES):
                part = jnp.maximum(part, s[:, c * LANES:(c + 1) * LANES])
            m = m_sc[hh]
            m_new = jnp.maximum(m, jnp.max(part, axis=-1, keepdims=True))
            p = jnp.exp2(s - jnp.concatenate([m_new] * (tq // LANES), axis=1))
            acc_sc[hh] = jnp.exp2(m - m_new) * acc_sc[hh] + jnp.dot(
                p.astype(BF16), vh[hh], preferred_element_type=F32)
            m_sc[hh] = m_new

    m_sc[...] = jnp.full_like(m_sc, -jnp.inf)
    acc_sc[...] = jnp.zeros_like(acc_sc)
    score(0, s0_sc)

    def pair(pp, carry):
        j = 2 * pp
        score(j + 1, s1_sc)
        consume(j, s0_sc, False)
        score(j + 2, s0_sc)
        consume(j + 1, s1_sc, False)
        return carry

    lax.fori_loop(0, qi // 2, pair, 0)

    @pl.when(qi % 2 == 1)
    def _():
        score(qi, s1_sc)
        consume(qi - 1, s0_sc, False)
        consume(qi, s1_sc, True)

    @pl.when(qi % 2 == 0)
    def _():
        consume(qi, s0_sc, True)

    a0, a1 = acc_sc[0], acc_sc[1]
    o_ref[0] = jnp.where(lane < HEAD_DIM, a0 / pltpu.roll(a0, HEAD_DIM, axis=1),
                         a1 / pltpu.roll(a1, HEAD_DIM, axis=1))


def _fox_attention(qb, kb, vb, cum_bht, tq):
    b, t, d = qb.shape
    hp = N_HEADS // 2
    nt = t // tq
    cum5 = cum_bht.reshape(b, hp, 2, nt, tq).transpose(0, 1, 3, 2, 4)
    return pl.pallas_call(
        functools.partial(_fox_attn_kernel, tq=tq),
        grid=(b, hp, nt),
        in_specs=[pl.BlockSpec((1, tq, LANES), lambda i, h, q: (i, q, h)),
                  pl.BlockSpec((1, t, LANES), lambda i, h, q: (i, 0, h)),
                  pl.BlockSpec((1, t, LANES), lambda i, h, q: (i, 0, h)),
                  pl.BlockSpec((1, 1, nt, 2, tq), lambda i, h, q: (i, h, 0, 0, 0))],
        out_specs=pl.BlockSpec((1, tq, LANES), lambda i, h, q: (i, q, h)),
        out_shape=jax.ShapeDtypeStruct((b, t, d), F32),
        scratch_shapes=[pltpu.VMEM((2, tq, tq), F32), pltpu.VMEM((2, tq, tq), F32),
                        pltpu.VMEM((2, tq, LANES), F32), pltpu.VMEM((2, tq, LANES), F32)],
        compiler_params=_cparams("arbitrary", "arbitrary", "arbitrary"),
        name="fox_prompt_attention",
    )(qb, kb, vb, cum5)


def _fox_decode_kernel(pt_ref, q_ref, *refs, n_q, n_par):
    del pt_ref
    k_refs, v_refs, lf_refs = refs[:n_par], refs[n_par:2 * n_par], refs[2 * n_par:3 * n_par]
    kn_ref, vn_ref, lfn_ref, tri_ref, o_ref, m_sc, l_sc, acc_sc, carry_sc = refs[3 * n_par:]
    p = pl.program_id(1)
    rows = n_q * N_HEADS

    @pl.when(p == 0)
    def _():
        m_sc[...] = jnp.full_like(m_sc, -jnp.inf)
        l_sc[...] = jnp.zeros_like(l_sc)
        acc_sc[...] = jnp.zeros_like(acc_sc)
        carry_sc[...] = jnp.zeros_like(carry_sc)

    q = q_ref[0]

    def cumulate(lft):
        cum = carry_sc[...] + _dot3(lft, tri_ref[0])
        carry_sc[...] = carry_sc[...] + _dot3(lft, tri_ref[1])
        return cum * LOG2E

    def update(kts, vts, cums, ok):
        ss = []
        for kt, cum in zip(kts, cums):
            s = jnp.dot(q, kt, preferred_element_type=F32) - jnp.concatenate([cum] * n_q, axis=0)
            ss.append(s if ok is None else jnp.where(ok, s, -jnp.inf))
        m = m_sc[...]
        m_new = m
        for s in ss:
            m_new = jnp.maximum(m_new, jnp.max(s, axis=-1, keepdims=True))
        alpha = jnp.exp2(m - m_new)
        l = alpha * l_sc[...]
        acc = alpha * acc_sc[...]
        for s, vt in zip(ss, vts):
            pr = jnp.exp2(s - m_new)
            l = l + jnp.sum(pr, axis=-1, keepdims=True)
            acc = acc + lax.dot_general(pr.astype(BF16), vt, (((1,), (1,)), ((), ())),
                                        preferred_element_type=F32)
        l_sc[...] = l
        acc_sc[...] = acc
        m_sc[...] = m_new

    update([r[0, 0].reshape(D_MODEL, PAGE_SIZE).astype(BF16) for r in k_refs],
           [r[0, 0].reshape(D_MODEL, PAGE_SIZE).astype(BF16) for r in v_refs],
           [cumulate(r[0, 0]) for r in lf_refs], None)

    @pl.when(p == pl.num_programs(1) - 1)
    def _():
        r = lax.broadcasted_iota(jnp.int32, (rows, PAGE_SIZE), 0)
        j = lax.broadcasted_iota(jnp.int32, (rows, PAGE_SIZE), 1)
        update([kn_ref[0]], [vn_ref[0]], [cumulate(lfn_ref[0])], j <= r // N_HEADS)
        o = acc_sc[...] / l_sc[...]
        rr = lax.broadcasted_iota(jnp.int32, (rows, D_MODEL), 0)
        cc = lax.broadcasted_iota(jnp.int32, (rows, D_MODEL), 1)
        o = jnp.where(rr % N_HEADS == cc // HEAD_DIM, o, 0.0)
        o_ref[0] = jnp.concatenate(
            [jnp.sum(o[N_HEADS * t:N_HEADS * (t + 1)], axis=0, keepdims=True) for t in range(n_q)],
            axis=0)


def _fox_decode(layer, page_table, q_bd, cache_kt, cache_vt, cache_lft, knt, vnt, lfn, tri):
    b, rows, d = q_bd.shape
    n_q = rows // N_HEADS
    n_pages = page_table.shape[1]
    n_par = DECODE_PAGES_PER_STEP if n_pages % DECODE_PAGES_PER_STEP == 0 else 1
    page5 = lambda u: (lambda i, p, pt: (layer, pt[i, p * n_par + u], 0, 0, 0))
    page4 = lambda u: (lambda i, p, pt: (layer, pt[i, p * n_par + u], 0, 0))
    per_b = lambda i, p, pt: (i, 0, 0)
    kv_specs = [pl.BlockSpec((1, 1, N_HEADS, HEAD_DIM, PAGE_SIZE), page5(u)) for u in range(n_par)]
    lf_specs = [pl.BlockSpec((1, 1, N_HEADS, PAGE_SIZE), page4(u)) for u in range(n_par)]
    grid_spec = pltpu.PrefetchScalarGridSpec(
        num_scalar_prefetch=1,
        grid=(b, n_pages // n_par),
        in_specs=[pl.BlockSpec((1, rows, d), per_b)] + kv_specs + kv_specs + lf_specs + [
            pl.BlockSpec((1, d, PAGE_SIZE), per_b),
            pl.BlockSpec((1, d, PAGE_SIZE), per_b),
            pl.BlockSpec((1, N_HEADS, PAGE_SIZE), per_b),
            pl.BlockSpec((2, LANES, LANES), lambda i, p, pt: (0, 0, 0))],
        out_specs=pl.BlockSpec((1, n_q, d), per_b),
        scratch_shapes=[pltpu.VMEM((rows, 1), F32), pltpu.VMEM((rows, 1), F32),
                        pltpu.VMEM((rows, d), F32), pltpu.VMEM((N_HEADS, LANES), F32)])
    return pl.pallas_call(
        functools.partial(_fox_decode_kernel, n_q=n_q, n_par=n_par),
        grid_spec=grid_spec,
        out_shape=jax.ShapeDtypeStruct((b, n_q, d), F32),
        compiler_params=_cparams("arbitrary", "arbitrary"),
        name="fox_sample_attention",
    )(page_table, q_bd, *([cache_kt] * n_par), *([cache_vt] * n_par), *([cache_lft] * n_par),
      knt, vnt, lfn, tri)


def _gated_out_kernel(o_ref, g_ref, res_ref, w_ref, y_ref):
    g = g_ref[...]
    z = o_ref[...] * (g * _sigmoid(g))
    y_ref[...] = res_ref[...] + _bdot(z, w_ref[...])


def _gated_out(o2d, g2d, res2d, w_out, tm):
    m, d = o2d.shape
    row = pl.BlockSpec((tm, d), lambda i: (i, 0))
    return pl.pallas_call(
        _gated_out_kernel,
        grid=(m // tm,),
        in_specs=[row, row, row, _const_spec((d, d))],
        out_specs=row,
        out_shape=jax.ShapeDtypeStruct((m, d), F32),
        compiler_params=_cparams("arbitrary"),
        name="gated_out_proj",
    )(o2d, g2d, res2d, w_out)


def _rwkv_pre_kernel(x_ref, xlast_ref, nw_ref, mu_ref, wr_ref, wk_ref, wv_ref, wg_ref, w1_ref,
                     w2_ref, a1_ref, a2_ref, vec_ref,
                     r_ref, dec_ref, k_ref, kk_ref, beta_ref, v_ref, g_ref, last_ref,
                     carry_ref, *, tm, t_last):
    @pl.when(pl.program_id(1) == 0)
    def _():
        carry_ref[...] = xlast_ref[0]

    xn = _rmsnorm_rows(x_ref[0], nw_ref[...])
    rowi = lax.broadcasted_iota(jnp.int32, (tm, D_MODEL), 0)
    prev = jnp.where(rowi == 0, carry_ref[...], pltpu.roll(xn, 1, axis=0))
    carry_ref[...] = xn[tm - 1:tm, :]
    last_ref[0] = xn[t_last:t_last + 1, :]
    dx = prev - xn
    mix = lambda j: xn + dx * mu_ref[j:j + 1, :]

    w0, a0, k_k, k_a = (vec_ref[i:i + 1, :] for i in range(4))
    r = _bdot(mix(0), wr_ref[...])
    k = _bdot(mix(1), wk_ref[...])
    v_ref[0] = _bdot(mix(2), wv_ref[...])
    g_ref[0] = _bdot(mix(3), wg_ref[...])
    wl = w0 + _bdot(jnp.tanh(_bdot(mix(4), w1_ref[...])), w2_ref[...])
    w_log = -(jnp.maximum(-wl, 0.0) + jnp.log(1.0 + jnp.exp(-jnp.abs(wl)))) - 0.5
    dec_ref[0] = jnp.exp(-jnp.exp(w_log))
    a = _sigmoid(a0 + _bdot(_bdot(mix(5), a1_ref[...]), a2_ref[...]))
    kk = k * k_k
    sq = kk * kk
    tot = sq[:, 0:LANES]
    for c in range(1, K_HI):
        tot = tot + sq[:, c * LANES:(c + 1) * LANES]
    inv = lax.rsqrt(jnp.maximum(_lane_group_sum(tot), L2_EPS_SQ))
    kkn = kk * jnp.concatenate([inv] * K_HI, axis=1)
    r_ref[0] = r
    k_ref[0] = k * (1.0 + (a - 1.0) * k_a)
    kk_ref[0] = kkn
    beta_ref[0] = kkn * a


def _rwkv_pre(x3, x_last, nw, mu8, wr, wk, wv, wg, w1, w2, a1, a2, vecs, tm, t_last):
    b, t, d = x3.shape
    tile = pl.BlockSpec((1, tm, d), lambda i, j: (i, j, 0))
    per_b = pl.BlockSpec((1, 1, d), lambda i, j: (i, 0, 0))
    f32o = jax.ShapeDtypeStruct((b, t, d), F32)
    return pl.pallas_call(
        functools.partial(_rwkv_pre_kernel, tm=tm, t_last=t_last),
        grid=(b, t // tm),
        in_specs=[tile, per_b, _const_spec((1, d)), _const_spec((8, d)),
                  _const_spec((d, d)), _const_spec((d, d)), _const_spec((d, d)),
                  _const_spec((d, d)), _const_spec((d, LANES)), _const_spec((LANES, d)),
                  _const_spec((d, LANES)), _const_spec((LANES, d)), _const_spec((8, d))],
        out_specs=[tile] * 7 + [per_b],
        out_shape=(f32o,) * 7 + (jax.ShapeDtypeStruct((b, 1, d), F32),),
        scratch_shapes=[pltpu.VMEM((1, d), F32)],
        compiler_params=_cparams("arbitrary", "arbitrary"),
        name="rwkv_pre",
    )(x3, x_last, nw, mu8, wr, wk, wv, wg, w1, w2, a1, a2, vecs)


def _wkv_kernel(r_ref, dec_ref, k_ref, kk_ref, beta_ref, v_ref, s0_ref, lnw_ref, lnb_ref, rk_ref,
                gs_ref, z_ref, sf_ref, *scs, nb, tb):
    s_scs, sa_scs, ya_scs, vc_scs = scs[:nb], scs[nb:2 * nb], scs[2 * nb:3 * nb], scs[3 * nb:]
    own_group = (lax.broadcasted_iota(jnp.int32, (SUBLANES, LANES), 1) // N_HEADS
                 == lax.broadcasted_iota(jnp.int32, (SUBLANES, LANES), 0))

    def group_sum(x):
        hi = x.astype(BF16)
        lo = (x - hi.astype(F32)).astype(BF16)
        return (jnp.dot(hi, gs_ref[...], preferred_element_type=F32)
                + jnp.dot(lo, gs_ref[...], preferred_element_type=F32))

    def expand(b, t):
        rows = [jnp.where(own_group, jnp.broadcast_to(v_ref[b, t, m:m + 1, :], (SUBLANES, LANES)), 0.0)
                for m in range(K_HI)]
        return group_sum(jnp.concatenate(rows, axis=0))

    def compact(x):
        rows = [jnp.sum(jnp.where(own_group, x[SUBLANES * m:SUBLANES * (m + 1)], 0.0), axis=0,
                        keepdims=True) for m in range(K_HI)]
        return jnp.concatenate(rows, axis=0)

    @pl.when(pl.program_id(1) == 0)
    def _():
        for b in range(nb):
            s_scs[b][...] = s0_ref[b]

    def rows_total(x):
        return _lane_group_sum(jnp.sum(x, axis=0, keepdims=True))

    def tree_sum(xs):
        while len(xs) > 1:
            xs = [xs[i] + xs[i + 1] for i in range(0, len(xs), 2)]
        return xs[0]

    def reduce_sa(b, t):
        return -group_sum(
            tree_sum([s_scs[b][c] * kk_ref[b, t, c:c + 1, :] for c in range(K_HI)]))

    def reduce_y(b, t):
        wr = dec_ref[b, t] * r_ref[b, t]
        ya_scs[b][...] = tree_sum([s_scs[b][c] * wr[c:c + 1, :] for c in range(K_HI)])

    def update(b, t, sa):
        w = dec_ref[b, t]
        km = k_ref[b, t]
        beta = beta_ref[b, t]
        vc = expand(b, t)
        for c in range(K_HI):
            s_scs[b][c] = (s_scs[b][c] * w[c:c + 1, :] + sa * beta[c:c + 1, :]
                           + vc * km[c:c + 1, :])
        sa_scs[b][...] = sa
        vc_scs[b][...] = vc

    def tail(b, t):
        r = r_ref[b, t]
        km = k_ref[b, t]
        vc = vc_scs[b][...]
        y = (group_sum(ya_scs[b][...]) + sa_scs[b][...] * rows_total(beta_ref[b, t] * r)
             + vc * rows_total(km * r))
        mean = jnp.sum(y, axis=0, keepdims=True) * (1.0 / HEAD_DIM)
        dy = y - mean
        var = jnp.sum(dy * dy, axis=0, keepdims=True) * (1.0 / HEAD_DIM)
        yn = dy * lax.rsqrt(var + GN_EPS)
        bonus = rows_total(r * km * rk_ref[...])
        z_ref[b, t] = compact(yn * lnw_ref[...] + lnb_ref[...] + vc * bonus)

    def step(t, carry):
        sas = [reduce_sa(b, t) for b in range(nb)]
        for b in range(nb):
            tail(b, t - 1)
        for b in range(nb):
            reduce_y(b, t)
        for b in range(nb):
            update(b, t, sas[b])
        return carry

    sas = [reduce_sa(b, 0) for b in range(nb)]
    for b in range(nb):
        reduce_y(b, 0)
    for b in range(nb):
        update(b, 0, sas[b])
    lax.fori_loop(1, tb, step, 0, unroll=3)
    for b in range(nb):
        tail(b, tb - 1)

    @pl.when(pl.program_id(1) == pl.num_programs(1) - 1)
    def _():
        for b in range(nb):
            sf_ref[b] = s_scs[b][...]


def _wkv_scan(r4, dec4, k4, kk4, beta4, v4, s0, lnw, lnb, rk, nb, tb):
    b, t = r4.shape[:2]
    lane = np.arange(LANES)
    same_head = jnp.asarray(lane[:, None] % N_HEADS == lane[None, :] % N_HEADS, dtype=BF16)
    rowf = pl.BlockSpec((nb, tb, K_HI, LANES), lambda i, j: (i, j, 0, 0))
    st = pl.BlockSpec((nb, K_HI, HEAD_DIM, LANES), lambda i, j: (i, 0, 0, 0))
    return pl.pallas_call(
        functools.partial(_wkv_kernel, nb=nb, tb=tb),
        grid=(b // nb, t // tb),
        in_specs=[rowf] * 6 + [st, _const_spec((HEAD_DIM, LANES)), _const_spec((HEAD_DIM, LANES)),
                               _const_spec((K_HI, LANES)), _const_spec((LANES, LANES))],
        out_specs=[rowf, st],
        out_shape=(jax.ShapeDtypeStruct((b, t, K_HI, LANES), F32),
                   jax.ShapeDtypeStruct((b, K_HI, HEAD_DIM, LANES), F32)),
        scratch_shapes=([pltpu.VMEM((K_HI, HEAD_DIM, LANES), F32) for _ in range(nb)]
                        + [pltpu.VMEM((HEAD_DIM, LANES), F32) for _ in range(3 * nb)]),
        compiler_params=_cparams("arbitrary", "arbitrary"),
        name="wkv_scan",
    )(r4, dec4, k4, kk4, beta4, v4, s0, lnw, lnb, rk, same_head)


def _pool_kernel(x_ref, hist_ref, nw_ref, win_ref, wgrp_ref, scale_ref, wout_ref,
                 y_ref, tail_ref, ext_ref, *, tm, t_valid, pos0):
    j = pl.program_id(1)

    @pl.when(j == 0)
    def _():
        ext_ref[0:POOL_HIST, :] = hist_ref[0]

    x = x_ref[0]
    xb = _rmsnorm_rows(x, nw_ref[...]).astype(BF16)
    u = jnp.dot(xb, win_ref[:, 0:D_MODEL], preferred_element_type=F32)
    g = jnp.dot(xb, win_ref[:, D_MODEL:2 * D_MODEL], preferred_element_type=F32)
    ext_ref[POOL_HIST:POOL_HIST + tm, :] = u
    pos = pos0 + j * tm + lax.broadcasted_iota(jnp.int32, (tm, POOL_GROUP), 0)
    mixed = []
    for gi, w in enumerate(POOL_WINDOWS):
        lo, hi = gi * POOL_GROUP, (gi + 1) * POOL_GROUP
        win = u[:, lo:hi]
        for dlt in range(1, w):
            win = win + ext_ref[POOL_HIST - dlt:POOL_HIST - dlt + tm, lo:hi]
        cnt = jnp.minimum(pos + 1, w).astype(F32)
        diff = win / cnt - u[:, lo:hi]
        mixed.append(_bdot(diff, wgrp_ref[gi]))
    mix = jnp.concatenate(mixed, axis=1) * scale_ref[...]
    z = mix * (g * _sigmoid(g))
    y_ref[0] = x + _bdot(z, wout_ref[...])
    tail = ext_ref[t_valid:t_valid + POOL_HIST, :]
    tail_ref[0] = tail
    ext_ref[0:POOL_HIST, :] = tail


def _pool_mixer(x3, hist, nw, w_in, w_grp, scale, w_out, tm, t_valid, pos0):
    b, t, d = x3.shape
    tile = pl.BlockSpec((1, tm, d), lambda i, j: (i, j, 0))
    per_b = pl.BlockSpec((1, POOL_HIST, d), lambda i, j: (i, 0, 0))
    return pl.pallas_call(
        functools.partial(_pool_kernel, tm=tm, t_valid=t_valid, pos0=pos0),
        grid=(b, t // tm),
        in_specs=[tile, per_b, _const_spec((1, d)), _const_spec((d, 2 * d)),
                  _const_spec((len(POOL_WINDOWS), POOL_GROUP, POOL_GROUP)),
                  _const_spec((1, d)), _const_spec((d, d))],
        out_specs=[tile, per_b],
        out_shape=(jax.ShapeDtypeStruct((b, t, d), F32),
                   jax.ShapeDtypeStruct((b, POOL_HIST, d), F32)),
        scratch_shapes=[pltpu.VMEM((POOL_HIST + tm, d), F32)],
        compiler_params=_cparams("arbitrary", "arbitrary"),
        name="pool_mixer",
    )(x3, hist, nw, w_in, w_grp, scale, w_out)


def _key_perm():
    n = np.arange(D_MODEL)
    kc, kl, h = n // LANES, (n % LANES) // N_HEADS, n % N_HEADS
    return h * HEAD_DIM + kc * SUBLANES + kl


def _value_rep_perm():
    n = np.arange(HEAD_DIM * LANES)
    return (n % N_HEADS) * HEAD_DIM + n // LANES


def _pad_rows(x3, rows):
    b, t, d = x3.shape
    return jnp.concatenate([x3, jnp.zeros((b, rows - t, d), x3.dtype)], axis=1)


def _state_to_tiles(s):
    b = s.shape[0]
    s = s.reshape(b, N_HEADS, HEAD_DIM, K_HI, SUBLANES)
    return s.transpose(0, 3, 2, 4, 1).reshape(b, K_HI, HEAD_DIM, LANES)


def _tiles_to_state(s):
    b = s.shape[0]
    s = s.reshape(b, K_HI, HEAD_DIM, SUBLANES, N_HEADS)
    return s.transpose(0, 4, 2, 1, 3).reshape(b, N_HEADS, HEAD_DIM, HEAD_DIM)


def _head_col_tile(vec):
    return jnp.tile(vec.reshape(N_HEADS, HEAD_DIM).T, (1, SUBLANES))


def _fox_layer(j, n_fox, stacked, y_p, y_s, nw, cache_kt, cache_vt, cache_lft, page_table,
               fox_w_in, fox_b_f, fox_q_gain, fox_k_gain, fox_w_out, tri):
    d, h = D_MODEL, N_HEADS
    bp, tp, _ = y_p.shape
    bs, ts, _ = y_s.shape
    w_in = fox_w_in[j]
    w_main = w_in[:, :4 * d].astype(BF16)
    w_f = jnp.pad(w_in[:, 4 * d:], ((0, 0), (0, LANES - h))).astype(BF16)
    b_f = jnp.pad(fox_b_f[j], (0, LANES - h)).reshape(1, LANES)
    qg = jnp.tile(fox_q_gain[j], h).reshape(1, d)
    kg = jnp.tile(fox_k_gain[j], h).reshape(1, d)
    gmat = jnp.asarray(np.kron(np.eye(h), np.ones((HEAD_DIM, HEAD_DIM))), dtype=BF16)
    w_out = fox_w_out[j].astype(BF16)

    xp = y_p.reshape(bp * tp, d)
    qb, kb, vb, g, kt_st, vt_st, lft_st = _fox_proj_prompt(
        j, n_fox, y_p, nw, w_main, w_f, b_f, qg, kg, gmat, stacked, tm=min(tp, 256))
    cum = _cumsum_time(lft_st[j], tri)
    sh = (bp, tp, d)
    o = _fox_attention(qb.reshape(sh), kb.reshape(sh), vb.reshape(sh), cum, tq=min(tp, 512))
    y_p_new = _gated_out(o.reshape(bp * tp, d), g, xp, w_out, tm=512).reshape(bp, tp, d)
    new_p = (kt_st, vt_st, lft_st)

    xs = y_s.reshape(bs * ts, d)
    qb, k, kb, v, vb, g, lf = _fox_proj(xs, nw, w_main, w_f, b_f, qg, kg, gmat, tm=bs * ts)
    lf_s = lf[:, :h].reshape(bs, ts, h)
    head_of_lane = np.arange(d) // HEAD_DIM
    head_mask = jnp.asarray(head_of_lane[None, :] == np.arange(h)[:, None])
    q_bd = jnp.where(head_mask[None, None], qb.reshape(bs, ts, 1, d), jnp.zeros((), BF16))
    q_bd = q_bd.reshape(bs, ts * h, d)
    as_page = lambda z: jnp.pad(z.reshape(bs, ts, d).transpose(0, 2, 1),
                                ((0, 0), (0, 0), (0, PAGE_SIZE - ts)))
    lfn = jnp.pad(lf_s.transpose(0, 2, 1), ((0, 0), (0, 0), (0, PAGE_SIZE - ts)))
    o = _fox_decode(j, page_table, q_bd, cache_kt, cache_vt, cache_lft, as_page(kb), as_page(vb),
                    lfn, tri)
    y_s_new = _gated_out(o.reshape(bs * ts, d), g, xs, w_out, tm=bs * ts).reshape(bs, ts, d)
    new_s = (k.reshape(bs, ts, h, HEAD_DIM), v.reshape(bs, ts, h, HEAD_DIM), lf_s)
    return y_p_new, y_s_new, new_p, new_s


def _rwkv_group(y3, x_last, s0_tiles, nw, prm, tm, t_valid, nb, tb):
    b, t, d = y3.shape
    (mu8, wr, wk, wv, wg, w1, w2, a1, a2, vecs, rk, lnw, lnb, w_out) = prm
    r, dec, k, kk, beta, v, g, last = _rwkv_pre(y3, x_last, nw, mu8, wr, wk, wv, wg, w1, w2, a1, a2,
                                                vecs, tm=tm, t_last=(t_valid - 1) % tm)
    m = b * t
    rowf = lambda z: z[:, :t_valid].reshape(b, t_valid, K_HI, LANES)
    z, s_fin = _wkv_scan(rowf(r), rowf(dec), rowf(k), rowf(kk), rowf(beta), rowf(v), s0_tiles,
                         lnw, lnb, rk, nb=nb, tb=tb)
    z = z.reshape(b, t_valid, d)
    if t_valid != t:
        z = _pad_rows(z, t)
    tmo = min(m, 512)
    y_new = _gated_out(z.reshape(m, d), g.reshape(m, d), y3.reshape(m, d), w_out, tm=tmo)
    return y_new.reshape(b, t, d), s_fin, last[:, 0]


def _rwkv_params(j, rwkv_mu, rwkv_w_rkvg, rwkv_w0, rwkv_w1, rwkv_w2, rwkv_a0, rwkv_a1, rwkv_a2,
                 rwkv_k_k, rwkv_k_a, rwkv_r_k, rwkv_ln_w, rwkv_ln_b, rwkv_w_out):
    d = D_MODEL
    perm = _key_perm()
    lora = rwkv_w1.shape[-1]
    mu8 = jnp.pad(rwkv_mu[j], ((0, 2), (0, 0)))
    wr = rwkv_w_rkvg[j, 0][:, perm].astype(BF16)
    wk = rwkv_w_rkvg[j, 1][:, perm].astype(BF16)
    wv = rwkv_w_rkvg[j, 2][:, perm].astype(BF16)
    wg = rwkv_w_rkvg[j, 3][:, perm].astype(BF16)
    w1 = jnp.pad(rwkv_w1[j], ((0, 0), (0, LANES - lora))).astype(BF16)
    w2 = jnp.pad(rwkv_w2[j][:, perm], ((0, LANES - lora), (0, 0))).astype(BF16)
    a1 = jnp.pad(rwkv_a1[j], ((0, 0), (0, LANES - lora))).astype(BF16)
    a2 = jnp.pad(rwkv_a2[j][:, perm], ((0, LANES - lora), (0, 0))).astype(BF16)
    vecs = jnp.stack([rwkv_w0[j][perm], rwkv_a0[j][perm], rwkv_k_k[j][perm], rwkv_k_a[j][perm]])
    vecs = jnp.pad(vecs, ((0, 4), (0, 0)))
    rk = rwkv_r_k[j].reshape(d)[perm].reshape(K_HI, LANES)
    lnw = _head_col_tile(rwkv_ln_w[j])
    lnb = _head_col_tile(rwkv_ln_b[j])
    w_out = rwkv_w_out[j][perm, :].astype(BF16)
    return (mu8, wr, wk, wv, wg, w1, w2, a1, a2, vecs, rk, lnw, lnb, w_out)


def kernel(x_prompt, x_sample, cache_k, cache_v, cache_logf, page_table, state_wkv, state_shift,
           state_pool, norm_w, fox_w_in, fox_b_f, fox_q_gain, fox_k_gain, fox_w_out,
           rwkv_mu, rwkv_w_rkvg, rwkv_w0, rwkv_w1, rwkv_w2, rwkv_a0, rwkv_a1, rwkv_a2,
           rwkv_k_k, rwkv_k_a, rwkv_r_k, rwkv_ln_w, rwkv_ln_b, rwkv_w_out,
           pool_w_in, pool_w_grp, pool_scale, pool_w_out):
    d = D_MODEL
    depth = norm_w.shape[0]
    bp, tp, _ = x_prompt.shape
    bs, ts, _ = x_sample.shape
    past = page_table.shape[1] * PAGE_SIZE
    n_fox, n_pool_pages = cache_k.shape[:2]
    cache_kt = cache_k.transpose(0, 1, 3, 4, 2)
    cache_vt = cache_v.transpose(0, 1, 3, 4, 2)
    cache_lft = cache_logf.transpose(0, 1, 3, 2)
    tri = _tri_mats()
    ts_pad = SUBLANES * pl.cdiv(ts, SUBLANES)

    y_p, y_s = x_prompt, x_sample
    ks, vs, ls = [], [], []
    wkv_p, sh_p, wkv_s, sh_s, pool_p, pool_s = [], [], [], [], [], []
    n_fox_layers = (depth + 2) // 3
    prompt_kv = None
    for i in range(depth):
        kind, j = i % 3, i // 3
        nw = norm_w[i].reshape(1, d)
        if kind == 0:
            y_p, y_s, prompt_kv, new_s = _fox_layer(
                j, n_fox_layers, prompt_kv, y_p, y_s, nw, cache_kt, cache_vt, cache_lft,
                page_table, fox_w_in, fox_b_f, fox_q_gain, fox_k_gain, fox_w_out, tri)
            ks.append(new_s[0]); vs.append(new_s[1]); ls.append(new_s[2])
        elif kind == 1:
            prm = _rwkv_params(j, rwkv_mu, rwkv_w_rkvg, rwkv_w0, rwkv_w1, rwkv_w2, rwkv_a0,
                               rwkv_a1, rwkv_a2, rwkv_k_k, rwkv_k_a, rwkv_r_k, rwkv_ln_w,
                               rwkv_ln_b, rwkv_w_out)
            zero_state = jnp.zeros((bp, K_HI, HEAD_DIM, LANES), F32)
            y_p, s_fin, last = _rwkv_group(y_p, jnp.zeros((bp, 1, d), F32), zero_state, nw, prm,
                                           tm=min(tp, 256), t_valid=tp, nb=bp, tb=min(tp, 64))
            wkv_p.append(_tiles_to_state(s_fin)); sh_p.append(last)
            y_s_pad, s_fin, last = _rwkv_group(
                _pad_rows(y_s, ts_pad), state_shift[j].reshape(bs, 1, d),
                _state_to_tiles(state_wkv[j]), nw, prm, tm=ts_pad, t_valid=ts, nb=2, tb=ts)
            y_s = y_s_pad[:, :ts]
            wkv_s.append(_tiles_to_state(s_fin)); sh_s.append(last)
        else:
            w_in = pool_w_in[j].astype(BF16)
            w_grp = pool_w_grp[j].astype(BF16)
            scale = pool_scale[j].reshape(1, d)
            w_out = pool_w_out[j].astype(BF16)
            y_p, tail = _pool_mixer(y_p, jnp.zeros((bp, POOL_HIST, d), F32), nw, w_in, w_grp,
                                    scale, w_out, tm=min(tp, 256), t_valid=min(tp, 256), pos0=0)
            pool_p.append(tail[:, 1:])
            hist = jnp.pad(state_pool[j], ((0, 0), (1, 0), (0, 0)))
            y_s_pad, tail = _pool_mixer(_pad_rows(y_s, ts_pad), hist, nw, w_in, w_grp, scale,
                                        w_out, tm=ts_pad, t_valid=ts, pos0=past)
            y_s = y_s_pad[:, :ts]
            pool_s.append(tail[:, 1:])
    st = lambda xs: jnp.stack(xs, 0)
    kt_st, vt_st, lft_st = prompt_kv
    to_thd = lambda z: z.transpose(0, 1, 4, 2, 3)
    return (y_p, y_s, to_thd(kt_st), to_thd(vt_st), lft_st.transpose(0, 1, 3, 2),
            st(ks), st(vs), st(ls),
            st(wkv_p), st(sh_p), st(wkv_s), st(sh_s), st(pool_p), st(pool_s))
```

```python
import functools

import jax
import jax.numpy as jnp
import numpy as np
from jax import lax
from jax.experimental import pallas as pl
from jax.experimental.pallas import tpu as pltpu

F32 = jnp.float32
BF16 = jnp.bfloat16

D_MODEL = 1024
HEAD_DIM = 64
N_HEADS = D_MODEL // HEAD_DIM
PAGE_SIZE = 128
POOL_WINDOWS = (2, 4, 8, 16)
POOL_GROUP = D_MODEL // len(POOL_WINDOWS)
POOL_HIST = 16
RMS_EPS = 1e-6
GN_EPS = 64e-5
L2_EPS_SQ = 1e-24
ATTN_SCALE = HEAD_DIM ** -0.5
LOG2E = 1.4426950408889634

LANES = 128
SUBLANES = 8
K_HI = HEAD_DIM // SUBLANES
VMEM_LIMIT = 56 * 1024 * 1024
DECODE_PAGES_PER_STEP = 16
ATTN_HEADS_PER_STEP = 2


def _cparams(*sem):
    return pltpu.CompilerParams(dimension_semantics=sem, vmem_limit_bytes=VMEM_LIMIT)


def _const_spec(shape):
    nd = len(shape)
    return pl.BlockSpec(shape, lambda *_: (0,) * nd, pipeline_mode=pl.Buffered(1))


def _rmsnorm_rows(x, w):
    ms = jnp.mean(x * x, axis=-1, keepdims=True)
    return x * lax.rsqrt(ms + RMS_EPS) * w


def _sigmoid(x):
    return 1.0 / (1.0 + jnp.exp(-x))


def _bdot(a, b):
    return jnp.dot(a.astype(BF16), b, preferred_element_type=F32)


def _dot3(x, m):
    hi = x.astype(BF16)
    r1 = x - hi.astype(F32)
    mid = r1.astype(BF16)
    lo = (r1 - mid.astype(F32)).astype(BF16)
    return (jnp.dot(hi, m, preferred_element_type=F32)
            + jnp.dot(mid, m, preferred_element_type=F32)
            + jnp.dot(lo, m, preferred_element_type=F32))


def _lane_group_sum(x):
    x = x + pltpu.roll(x, 16, axis=x.ndim - 1)
    x = x + pltpu.roll(x, 32, axis=x.ndim - 1)
    return x + pltpu.roll(x, 64, axis=x.ndim - 1)


def _fox_proj_kernel(x_ref, nw_ref, w_ref, wf_ref, bf_ref, qg_ref, kg_ref, gm_ref, *refs,
                     time_minor):
    if time_minor:
        qb_ref, kb_ref, vb_ref, g_ref, kt_ref, vt_ref, lft_ref = refs[-7:]
    else:
        qb_ref, k_ref, kb_ref, v_ref, vb_ref, g_ref, lf_ref = refs
    xb = _rmsnorm_rows(x_ref[...], nw_ref[...]).astype(BF16)

    def headnorm(z, gain):
        ms = _bdot(z * z, gm_ref[...]) * (1.0 / HEAD_DIM)
        return z * lax.rsqrt(ms + RMS_EPS) * gain

    d = D_MODEL
    q = headnorm(jnp.dot(xb, w_ref[:, 0:d], preferred_element_type=F32), qg_ref[...])
    qb_ref[...] = (q * (ATTN_SCALE * LOG2E)).astype(BF16)
    k = headnorm(jnp.dot(xb, w_ref[:, d:2 * d], preferred_element_type=F32), kg_ref[...])
    kb_ref[...] = k.astype(BF16)
    v = jnp.dot(xb, w_ref[:, 2 * d:3 * d], preferred_element_type=F32)
    vb_ref[...] = v.astype(BF16)
    g_ref[...] = jnp.dot(xb, w_ref[:, 3 * d:4 * d], preferred_element_type=F32)
    fl = jnp.dot(xb, wf_ref[...], preferred_element_type=F32) + bf_ref[...]
    lf = -(jnp.maximum(-fl, 0.0) + jnp.log(1.0 + jnp.exp(-jnp.abs(fl))))
    if time_minor:
        tm = k.shape[0]
        kt_ref[0, 0] = k.T.reshape(N_HEADS, HEAD_DIM, tm)
        vt_ref[0, 0] = v.T.reshape(N_HEADS, HEAD_DIM, tm)
        lft_ref[0, 0] = lf.T[0:N_HEADS, :]
    else:
        k_ref[...] = k
        v_ref[...] = v
        lf_ref[...] = lf


def _fox_proj(x2d, nw, w_main, w_f, b_f, qg, kg, gmat, tm):
    m = x2d.shape[0]
    d = D_MODEL
    row = lambda n: pl.BlockSpec((tm, n), lambda i: (i, 0))
    outs = (jax.ShapeDtypeStruct((m, d), BF16), jax.ShapeDtypeStruct((m, d), F32),
            jax.ShapeDtypeStruct((m, d), BF16), jax.ShapeDtypeStruct((m, d), F32),
            jax.ShapeDtypeStruct((m, d), BF16), jax.ShapeDtypeStruct((m, d), F32),
            jax.ShapeDtypeStruct((m, LANES), F32))
    return pl.pallas_call(
        functools.partial(_fox_proj_kernel, time_minor=False),
        grid=(m // tm,),
        in_specs=[row(d), _const_spec((1, d)), _const_spec((d, 4 * d)), _const_spec((d, LANES)),
                  _const_spec((1, LANES)), _const_spec((1, d)), _const_spec((1, d)),
                  _const_spec((d, d))],
        out_specs=[row(d)] * 6 + [row(LANES)],
        out_shape=outs,
        compiler_params=_cparams("arbitrary"),
        name="fox_proj",
    )(x2d, nw, w_main, w_f, b_f, qg, kg, gmat)


def _fox_proj_prompt(layer, n_layers, x3, nw, w_main, w_f, b_f, qg, kg, gmat, stacked, tm):
    b, t, d = x3.shape
    nt = t // tm
    m = b * t
    row = pl.BlockSpec((tm, d), lambda i: (i, 0))
    kv_spec = pl.BlockSpec((1, 1, N_HEADS, HEAD_DIM, tm), lambda i: (layer, i // nt, 0, 0, i % nt))
    lf_spec = pl.BlockSpec((1, 1, N_HEADS, tm), lambda i: (layer, i // nt, 0, i % nt))
    kv_shape = jax.ShapeDtypeStruct((n_layers, b, N_HEADS, HEAD_DIM, t), F32)
    outs = (jax.ShapeDtypeStruct((m, d), BF16), jax.ShapeDtypeStruct((m, d), BF16),
            jax.ShapeDtypeStruct((m, d), BF16), jax.ShapeDtypeStruct((m, d), F32),
            kv_shape, kv_shape, jax.ShapeDtypeStruct((n_layers, b, N_HEADS, t), F32))
    in_specs = [row, _const_spec((1, d)), _const_spec((d, 4 * d)), _const_spec((d, LANES)),
                _const_spec((1, LANES)), _const_spec((1, d)), _const_spec((1, d)),
                _const_spec((d, d))]
    args = [x3.reshape(m, d), nw, w_main, w_f, b_f, qg, kg, gmat]
    aliases = {}
    if stacked is not None:
        in_specs += [pl.BlockSpec(memory_space=pl.ANY)] * 3
        aliases = {len(args) + i: 4 + i for i in range(3)}
        args += list(stacked)
    return pl.pallas_call(
        functools.partial(_fox_proj_kernel, time_minor=True),
        grid=(m // tm,),
        in_specs=in_specs,
        out_specs=[row] * 4 + [kv_spec, kv_spec, lf_spec],
        out_shape=outs,
        input_output_aliases=aliases,
        compiler_params=_cparams("arbitrary"),
        name="fox_proj_prompt",
    )(*args)


def _cumsum_kernel(x_ref, tri_ref, o_ref, carry_ref, *, nchunk):
    @pl.when(pl.program_id(1) == 0)
    def _():
        carry_ref[...] = jnp.zeros_like(carry_ref)

    carry = carry_ref[...]
    upper = tri_ref[0]
    ones = tri_ref[1]
    for c in range(nchunk):
        x = x_ref[0, :, c * LANES:(c + 1) * LANES]
        o_ref[0, :, c * LANES:(c + 1) * LANES] = (carry + _dot3(x, upper)) * LOG2E
        carry = carry + _dot3(x, ones)
    carry_ref[...] = carry


def _tri_mats():
    i = np.arange(LANES)
    upper = (i[:, None] <= i[None, :]).astype(np.float32)
    return jnp.asarray(np.stack([upper, np.ones_like(upper)]), dtype=BF16)


def _cumsum_time(x_bht, tri):
    b, h, t = x_bht.shape
    tc = min(t, 1024)
    return pl.pallas_call(
        functools.partial(_cumsum_kernel, nchunk=tc // LANES),
        grid=(b, t // tc),
        in_specs=[pl.BlockSpec((1, h, tc), lambda i, j: (i, 0, j)), _const_spec((2, LANES, LANES))],
        out_specs=pl.BlockSpec((1, h, tc), lambda i, j: (i, 0, j)),
        out_shape=jax.ShapeDtypeStruct((b, h, t), F32),
        scratch_shapes=[pltpu.VMEM((h, LANES), F32)],
        compiler_params=_cparams("arbitrary", "arbitrary"),
        name="logf_cumsum",
    )(x_bht, tri)


def _fox_attn_kernel(q_ref, k_ref, v_ref, c_ref, o_ref, s0_sc, s1_sc, m_sc, acc_sc, *, tq, nh):
    qi = pl.program_id(2)
    lane = lax.broadcasted_iota(jnp.int32, (tq, LANES), 1)
    halves = (lane < HEAD_DIM, lane >= HEAD_DIM)
    group = lambda ref_row, hh: ref_row[:, (hh // 2) * LANES:(hh // 2 + 1) * LANES]
    q_all = q_ref[0]
    qh = [jnp.where(halves[hh % 2], group(q_all, hh), jnp.zeros((), BF16)) for hh in range(nh)]

    def score(kj, dst):
        k_all = k_ref[0, pl.ds(pl.multiple_of(kj * tq, tq), tq), :]
        for hh in range(nh):
            dst[hh] = lax.dot_general(qh[hh], group(k_all, hh), (((1,), (1,)), ((), ())),
                                      preferred_element_type=F32)

    def consume(kj, src, diagonal):
        v_all = v_ref[0, pl.ds(pl.multiple_of(kj * tq, tq), tq), :]
        vh = [jnp.where(halves[hh % 2], group(v_all, hh), jnp.ones((), BF16)) for hh in range(nh)]
        cj = c_ref[0, 0, kj]
        for hh in range(nh):
            s = src[hh] - cj[hh:hh + 1, :]
            if diagonal:
                row = lax.broadcasted_iota(jnp.int32, (tq, tq), 0)
                col = lax.broadcasted_iota(jnp.int32, (tq, tq), 1)
                s = jnp.where(row >= col, s, -jnp.inf)
            part = s[:, 0:LANES]
            for c in range(1, tq // LANES):
                part = jnp.maximum(part, s[:, c * LANES:(c + 1) * LANES])
            m = m_sc[hh]
            m_new = jnp.maximum(m, jnp.max(part, axis=-1, keepdims=True))
            p = jnp.exp2(s - jnp.concatenate([m_new] * (tq // LANES), axis=1))
            acc_sc[hh] = jnp.exp2(m - m_new) * acc_sc[hh] + jnp.dot(
                p.astype(BF16), vh[hh], preferred_element_type=F32)
            m_sc[hh] = m_new

    m_sc[...] = jnp.full_like(m_sc, -jnp.inf)
    acc_sc[...] = jnp.zeros_like(acc_sc)
    score(0, s0_sc)

    def pair(pp, carry):
        j = 2 * pp
        score(j + 1, s1_sc)
        consume(j, s0_sc, False)
        score(j + 2, s0_sc)
        consume(j + 1, s1_sc, False)
        return carry

    lax.fori_loop(0, qi // 2, pair, 0)

    @pl.when(qi % 2 == 1)
    def _():
        score(qi, s1_sc)
        consume(qi - 1, s0_sc, False)
        consume(qi, s1_sc, True)

    @pl.when(qi % 2 == 0)
    def _():
        consume(qi, s0_sc, True)

    outs = []
    for g in range(nh // 2):
        a0, a1 = acc_sc[2 * g], acc_sc[2 * g + 1]
        outs.append(jnp.where(halves[0], a0 / pltpu.roll(a0, HEAD_DIM, axis=1),
                              a1 / pltpu.roll(a1, HEAD_DIM, axis=1)))
    o_ref[0] = jnp.concatenate(outs, axis=1)


def _fox_attention(qb, kb, vb, cum_bht, tq):
    b, t, d = qb.shape
    nh = ATTN_HEADS_PER_STEP
    hg = N_HEADS // nh
    w = nh * HEAD_DIM
    nt = t // tq
    cum5 = cum_bht.reshape(b, hg, nh, nt, tq).transpose(0, 1, 3, 2, 4)
    return pl.pallas_call(
        functools.partial(_fox_attn_kernel, tq=tq, nh=nh),
        grid=(b, hg, nt),
        in_specs=[pl.BlockSpec((1, tq, w), lambda i, h, q: (i, q, h)),
                  pl.BlockSpec((1, t, w), lambda i, h, q: (i, 0, h)),
                  pl.BlockSpec((1, t, w), lambda i, h, q: (i, 0, h)),
                  pl.BlockSpec((1, 1, nt, nh, tq), lambda i, h, q: (i, h, 0, 0, 0))],
        out_specs=pl.BlockSpec((1, tq, w), lambda i, h, q: (i, q, h)),
        out_shape=jax.ShapeDtypeStruct((b, t, d), F32),
        scratch_shapes=[pltpu.VMEM((nh, tq, tq), F32), pltpu.VMEM((nh, tq, tq), F32),
                        pltpu.VMEM((nh, tq, LANES), F32), pltpu.VMEM((nh, tq, LANES), F32)],
        compiler_params=_cparams("arbitrary", "arbitrary", "arbitrary"),
        name="fox_prompt_attention",
    )(qb, kb, vb, cum5)


def _fox_decode_kernel(pt_ref, q_ref, *refs, n_q, n_par):
    del pt_ref
    k_refs, v_refs, lf_refs = refs[:n_par], refs[n_par:2 * n_par], refs[2 * n_par:3 * n_par]
    kn_ref, vn_ref, lfn_ref, tri_ref, o_ref, m_sc, l_sc, acc_sc, carry_sc = refs[3 * n_par:]
    p = pl.program_id(1)
    rows = n_q * N_HEADS

    @pl.when(p == 0)
    def _():
        m_sc[...] = jnp.full_like(m_sc, -jnp.inf)
        l_sc[...] = jnp.zeros_like(l_sc)
        acc_sc[...] = jnp.zeros_like(acc_sc)
        carry_sc[...] = jnp.zeros_like(carry_sc)

    q = q_ref[0]

    def cumulate(lft):
        cum = carry_sc[...] + _dot3(lft, tri_ref[0])
        carry_sc[...] = carry_sc[...] + _dot3(lft, tri_ref[1])
        return cum * LOG2E

    def update(kts, vts, cums, ok):
        ss = []
        for kt, cum in zip(kts, cums):
            s = jnp.dot(q, kt, preferred_element_type=F32) - jnp.concatenate([cum] * n_q, axis=0)
            ss.append(s if ok is None else jnp.where(ok, s, -jnp.inf))
        m = m_sc[...]
        m_new = m
        for s in ss:
            m_new = jnp.maximum(m_new, jnp.max(s, axis=-1, keepdims=True))
        alpha = jnp.exp2(m - m_new)
        l = alpha * l_sc[...]
        acc = alpha * acc_sc[...]
        for s, vt in zip(ss, vts):
            pr = jnp.exp2(s - m_new)
            l = l + jnp.sum(pr, axis=-1, keepdims=True)
            acc = acc + lax.dot_general(pr.astype(BF16), vt, (((1,), (1,)), ((), ())),
                                        preferred_element_type=F32)
        l_sc[...] = l
        acc_sc[...] = acc
        m_sc[...] = m_new

    update([r[0, 0].reshape(D_MODEL, PAGE_SIZE).astype(BF16) for r in k_refs],
           [r[0, 0].reshape(D_MODEL, PAGE_SIZE).astype(BF16) for r in v_refs],
           [cumulate(r[0, 0]) for r in lf_refs], None)

    @pl.when(p == pl.num_programs(1) - 1)
    def _():
        r = lax.broadcasted_iota(jnp.int32, (rows, PAGE_SIZE), 0)
        j = lax.broadcasted_iota(jnp.int32, (rows, PAGE_SIZE), 1)
        update([kn_ref[0]], [vn_ref[0]], [cumulate(lfn_ref[0])], j <= r // N_HEADS)
        o = acc_sc[...] / l_sc[...]
        rr = lax.broadcasted_iota(jnp.int32, (rows, D_MODEL), 0)
        cc = lax.broadcasted_iota(jnp.int32, (rows, D_MODEL), 1)
        o = jnp.where(rr % N_HEADS == cc // HEAD_DIM, o, 0.0)
        o_ref[0] = jnp.concatenate(
            [jnp.sum(o[N_HEADS * t:N_HEADS * (t + 1)], axis=0, keepdims=True) for t in range(n_q)],
            axis=0)


def _fox_decode(layer, page_table, q_bd, cache_kt, cache_vt, cache_lft, knt, vnt, lfn, tri):
    b, rows, d = q_bd.shape
    n_q = rows // N_HEADS
    n_pages = page_table.shape[1]
    n_par = DECODE_PAGES_PER_STEP if n_pages % DECODE_PAGES_PER_STEP == 0 else 1
    page5 = lambda u: (lambda i, p, pt: (layer, pt[i, p * n_par + u], 0, 0, 0))
    page4 = lambda u: (lambda i, p, pt: (layer, pt[i, p * n_par + u], 0, 0))
    per_b = lambda i, p, pt: (i, 0, 0)
    kv_specs = [pl.BlockSpec((1, 1, N_HEADS, HEAD_DIM, PAGE_SIZE), page5(u)) for u in range(n_par)]
    lf_specs = [pl.BlockSpec((1, 1, N_HEADS, PAGE_SIZE), page4(u)) for u in range(n_par)]
    grid_spec = pltpu.PrefetchScalarGridSpec(
        num_scalar_prefetch=1,
        grid=(b, n_pages // n_par),
        in_specs=[pl.BlockSpec((1, rows, d), per_b)] + kv_specs + kv_specs + lf_specs + [
            pl.BlockSpec((1, d, PAGE_SIZE), per_b),
            pl.BlockSpec((1, d, PAGE_SIZE), per_b),
            pl.BlockSpec((1, N_HEADS, PAGE_SIZE), per_b),
            pl.BlockSpec((2, LANES, LANES), lambda i, p, pt: (0, 0, 0))],
        out_specs=pl.BlockSpec((1, n_q, d), per_b),
        scratch_shapes=[pltpu.VMEM((rows, 1), F32), pltpu.VMEM((rows, 1), F32),
                        pltpu.VMEM((rows, d), F32), pltpu.VMEM((N_HEADS, LANES), F32)])
    return pl.pallas_call(
        functools.partial(_fox_decode_kernel, n_q=n_q, n_par=n_par),
        grid_spec=grid_spec,
        out_shape=jax.ShapeDtypeStruct((b, n_q, d), F32),
        compiler_params=_cparams("arbitrary", "arbitrary"),
        name="fox_sample_attention",
    )(page_table, q_bd, *([cache_kt] * n_par), *([cache_vt] * n_par), *([cache_lft] * n_par),
      knt, vnt, lfn, tri)


def _gated_out_kernel(o_ref, g_ref, res_ref, w_ref, y_ref):
    g = g_ref[...]
    z = o_ref[...] * (g * _sigmoid(g))
    y_ref[...] = res_ref[...] + _bdot(z, w_ref[...])


def _gated_out(o2d, g2d, res2d, w_out, tm):
    m, d = o2d.shape
    row = pl.BlockSpec((tm, d), lambda i: (i, 0))
    return pl.pallas_call(
        _gated_out_kernel,
        grid=(m // tm,),
        in_specs=[row, row, row, _const_spec((d, d))],
        out_specs=row,
        out_shape=jax.ShapeDtypeStruct((m, d), F32),
        compiler_params=_cparams("arbitrary"),
        name="gated_out_proj",
    )(o2d, g2d, res2d, w_out)


def _rwkv_pre_kernel(x_ref, xlast_ref, nw_ref, mu_ref, wr_ref, wk_ref, wv_ref, wg_ref, w1_ref,
                     w2_ref, a1_ref, a2_ref, vec_ref,
                     r_ref, dec_ref, k_ref, kk_ref, beta_ref, v_ref, g_ref, last_ref,
                     carry_ref, *, tm, t_last):
    @pl.when(pl.program_id(1) == 0)
    def _():
        carry_ref[...] = xlast_ref[0]

    xn = _rmsnorm_rows(x_ref[0], nw_ref[...])
    rowi = lax.broadcasted_iota(jnp.int32, (tm, D_MODEL), 0)
    prev = jnp.where(rowi == 0, carry_ref[...], pltpu.roll(xn, 1, axis=0))
    carry_ref[...] = xn[tm - 1:tm, :]
    last_ref[0] = xn[t_last:t_last + 1, :]
    dx = prev - xn
    mix = lambda j: xn + dx * mu_ref[j:j + 1, :]

    w0, a0, k_k, k_a = (vec_ref[i:i + 1, :] for i in range(4))
    r = _bdot(mix(0), wr_ref[...])
    k = _bdot(mix(1), wk_ref[...])
    v_ref[0] = _bdot(mix(2), wv_ref[...])
    g_ref[0] = _bdot(mix(3), wg_ref[...])
    wl = w0 + _bdot(jnp.tanh(_bdot(mix(4), w1_ref[...])), w2_ref[...])
    w_log = -(jnp.maximum(-wl, 0.0) + jnp.log(1.0 + jnp.exp(-jnp.abs(wl)))) - 0.5
    dec_ref[0] = jnp.exp(-jnp.exp(w_log))
    a = _sigmoid(a0 + _bdot(_bdot(mix(5), a1_ref[...]), a2_ref[...]))
    kk = k * k_k
    sq = kk * kk
    tot = sq[:, 0:LANES]
    for c in range(1, K_HI):
        tot = tot + sq[:, c * LANES:(c + 1) * LANES]
    inv = lax.rsqrt(jnp.maximum(_lane_group_sum(tot), L2_EPS_SQ))
    kkn = kk * jnp.concatenate([inv] * K_HI, axis=1)
    r_ref[0] = r
    k_ref[0] = k * (1.0 + (a - 1.0) * k_a)
    kk_ref[0] = kkn
    beta_ref[0] = kkn * a


def _rwkv_pre(x3, x_last, nw, mu8, wr, wk, wv, wg, w1, w2, a1, a2, vecs, tm, t_last):
    b, t, d = x3.shape
    tile = pl.BlockSpec((1, tm, d), lambda i, j: (i, j, 0))
    per_b = pl.BlockSpec((1, 1, d), lambda i, j: (i, 0, 0))
    f32o = jax.ShapeDtypeStruct((b, t, d), F32)
    return pl.pallas_call(
        functools.partial(_rwkv_pre_kernel, tm=tm, t_last=t_last),
        grid=(b, t // tm),
        in_specs=[tile, per_b, _const_spec((1, d)), _const_spec((8, d)),
                  _const_spec((d, d)), _const_spec((d, d)), _const_spec((d, d)),
                  _const_spec((d, d)), _const_spec((d, LANES)), _const_spec((LANES, d)),
                  _const_spec((d, LANES)), _const_spec((LANES, d)), _const_spec((8, d))],
        out_specs=[tile] * 7 + [per_b],
        out_shape=(f32o,) * 7 + (jax.ShapeDtypeStruct((b, 1, d), F32),),
        scratch_shapes=[pltpu.VMEM((1, d), F32)],
        compiler_params=_cparams("arbitrary", "arbitrary"),
        name="rwkv_pre",
    )(x3, x_last, nw, mu8, wr, wk, wv, wg, w1, w2, a1, a2, vecs)


def _wkv_kernel(r_ref, dec_ref, k_ref, kk_ref, beta_ref, v_ref, s0_ref, lnw_ref, lnb_ref, rk_ref,
                gs_ref, z_ref, sf_ref, *scs, nb, tb):
    s_scs, sa_scs, ya_scs, vc_scs, pend_scs = (scs[i * nb:(i + 1) * nb] for i in range(5))
    own_group = (lax.broadcasted_iota(jnp.int32, (SUBLANES, LANES), 1) // N_HEADS
                 == lax.broadcasted_iota(jnp.int32, (SUBLANES, LANES), 0))

    def group_sum(x):
        hi = x.astype(BF16)
        lo = (x - hi.astype(F32)).astype(BF16)
        return (jnp.dot(hi, gs_ref[...], preferred_element_type=F32)
                + jnp.dot(lo, gs_ref[...], preferred_element_type=F32))

    def expand(b, t):
        rows = [jnp.where(own_group, jnp.broadcast_to(v_ref[b, t, m:m + 1, :], (SUBLANES, LANES)), 0.0)
                for m in range(K_HI)]
        return group_sum(jnp.concatenate(rows, axis=0))

    def compact(x):
        rows = [jnp.sum(jnp.where(own_group, x[SUBLANES * m:SUBLANES * (m + 1)], 0.0), axis=0,
                        keepdims=True) for m in range(K_HI)]
        return jnp.concatenate(rows, axis=0)

    @pl.when(pl.program_id(1) == 0)
    def _():
        for b in range(nb):
            s_scs[b][...] = s0_ref[b]

    def rows_total(x):
        return _lane_group_sum(jnp.sum(x, axis=0, keepdims=True))

    def tree_sum(xs):
        while len(xs) > 1:
            xs = [xs[i] + xs[i + 1] for i in range(0, len(xs), 2)]
        return xs[0]

    def reduce_sa(b, t):
        return -group_sum(
            tree_sum([s_scs[b][c] * kk_ref[b, t, c:c + 1, :] for c in range(K_HI)]))

    def reduce_y(b, t):
        wr = dec_ref[b, t] * r_ref[b, t]
        ya_scs[b][...] = tree_sum([s_scs[b][c] * wr[c:c + 1, :] for c in range(K_HI)])

    def update(b, t, sa):
        w = dec_ref[b, t]
        km = k_ref[b, t]
        beta = beta_ref[b, t]
        vc = expand(b, t)
        for c in range(K_HI):
            s_scs[b][c] = (s_scs[b][c] * w[c:c + 1, :] + sa * beta[c:c + 1, :]
                           + vc * km[c:c + 1, :])
        sa_scs[b][...] = sa
        vc_scs[b][...] = vc

    def tail(b, t):
        r = r_ref[b, t]
        km = k_ref[b, t]
        vc = vc_scs[b][...]
        y = (group_sum(ya_scs[b][...]) + sa_scs[b][...] * rows_total(beta_ref[b, t] * r)
             + vc * rows_total(km * r))
        mean = jnp.sum(y, axis=0, keepdims=True) * (1.0 / HEAD_DIM)
        dy = y - mean
        var = jnp.sum(dy * dy, axis=0, keepdims=True) * (1.0 / HEAD_DIM)
        yn = dy * lax.rsqrt(var + GN_EPS)
        bonus = rows_total(r * km * rk_ref[...])
        z_ref[b, t] = compact(yn * lnw_ref[...] + lnb_ref[...] + vc * bonus)

    def step(t, carry):
        sa0 = reduce_sa(0, t)
        for b in range(1, nb):
            update(b, t - 1, pend_scs[b][...])
        for b in range(nb):
            tail(b, t - 1)
        reduce_y(0, t)
        for b in range(1, nb):
            pend_scs[b][...] = reduce_sa(b, t)
        update(0, t, sa0)
        for b in range(1, nb):
            reduce_y(b, t)
        return carry

    sa0 = reduce_sa(0, 0)
    for b in range(1, nb):
        pend_scs[b][...] = reduce_sa(b, 0)
    for b in range(nb):
        reduce_y(b, 0)
    update(0, 0, sa0)
    lax.fori_loop(1, tb, step, 0, unroll=3)
    for b in range(1, nb):
        update(b, tb - 1, pend_scs[b][...])
    for b in range(nb):
        tail(b, tb - 1)

    @pl.when(pl.program_id(1) == pl.num_programs(1) - 1)
    def _():
        for b in range(nb):
            sf_ref[b] = s_scs[b][...]


def _wkv_scan(r4, dec4, k4, kk4, beta4, v4, s0, lnw, lnb, rk, nb, tb):
    b, t = r4.shape[:2]
    lane = np.arange(LANES)
    same_head = jnp.asarray(lane[:, None] % N_HEADS == lane[None, :] % N_HEADS, dtype=BF16)
    rowf = pl.BlockSpec((nb, tb, K_HI, LANES), lambda i, j: (i, j, 0, 0))
    st = pl.BlockSpec((nb, K_HI, HEAD_DIM, LANES), lambda i, j: (i, 0, 0, 0))
    return pl.pallas_call(
        functools.partial(_wkv_kernel, nb=nb, tb=tb),
        grid=(b // nb, t // tb),
        in_specs=[rowf] * 6 + [st, _const_spec((HEAD_DIM, LANES)), _const_spec((HEAD_DIM, LANES)),
                               _const_spec((K_HI, LANES)), _const_spec((LANES, LANES))],
        out_specs=[rowf, st],
        out_shape=(jax.ShapeDtypeStruct((b, t, K_HI, LANES), F32),
                   jax.ShapeDtypeStruct((b, K_HI, HEAD_DIM, LANES), F32)),
        scratch_shapes=([pltpu.VMEM((K_HI, HEAD_DIM, LANES), F32) for _ in range(nb)]
                        + [pltpu.VMEM((HEAD_DIM, LANES), F32) for _ in range(4 * nb)]),
        compiler_params=_cparams("arbitrary", "arbitrary"),
        name="wkv_scan",
    )(r4, dec4, k4, kk4, beta4, v4, s0, lnw, lnb, rk, same_head)


def _pool_kernel(x_ref, hist_ref, nw_ref, win_ref, wgrp_ref, scale_ref, wout_ref,
                 y_ref, tail_ref, ext_ref, *, tm, t_valid, pos0):
    j = pl.program_id(1)

    @pl.when(j == 0)
    def _():
        ext_ref[0:POOL_HIST, :] = hist_ref[0]

    x = x_ref[0]
    xb = _rmsnorm_rows(x, nw_ref[...]).astype(BF16)
    u = jnp.dot(xb, win_ref[:, 0:D_MODEL], preferred_element_type=F32)
    g = jnp.dot(xb, win_ref[:, D_MODEL:2 * D_MODEL], preferred_element_type=F32)
    ext_ref[POOL_HIST:POOL_HIST + tm, :] = u
    pos = pos0 + j * tm + lax.broadcasted_iota(jnp.int32, (tm, POOL_GROUP), 0)
    mixed = []
    for gi, w in enumerate(POOL_WINDOWS):
        lo, hi = gi * POOL_GROUP, (gi + 1) * POOL_GROUP
        win = u[:, lo:hi]
        for dlt in range(1, w):
            win = win + ext_ref[POOL_HIST - dlt:POOL_HIST - dlt + tm, lo:hi]
        cnt = jnp.minimum(pos + 1, w).astype(F32)
        diff = win / cnt - u[:, lo:hi]
        mixed.append(_bdot(diff, wgrp_ref[gi]))
    mix = jnp.concatenate(mixed, axis=1) * scale_ref[...]
    z = mix * (g * _sigmoid(g))
    y_ref[0] = x + _bdot(z, wout_ref[...])
    tail = ext_ref[t_valid:t_valid + POOL_HIST, :]
    tail_ref[0] = tail
    ext_ref[0:POOL_HIST, :] = tail


def _pool_mixer(x3, hist, nw, w_in, w_grp, scale, w_out, tm, t_valid, pos0):
    b, t, d = x3.shape
    tile = pl.BlockSpec((1, tm, d), lambda i, j: (i, j, 0))
    per_b = pl.BlockSpec((1, POOL_HIST, d), lambda i, j: (i, 0, 0))
    return pl.pallas_call(
        functools.partial(_pool_kernel, tm=tm, t_valid=t_valid, pos0=pos0),
        grid=(b, t // tm),
        in_specs=[tile, per_b, _const_spec((1, d)), _const_spec((d, 2 * d)),
                  _const_spec((len(POOL_WINDOWS), POOL_GROUP, POOL_GROUP)),
                  _const_spec((1, d)), _const_spec((d, d))],
        out_specs=[tile, per_b],
        out_shape=(jax.ShapeDtypeStruct((b, t, d), F32),
                   jax.ShapeDtypeStruct((b, POOL_HIST, d), F32)),
        scratch_shapes=[pltpu.VMEM((POOL_HIST + tm, d), F32)],
        compiler_params=_cparams("arbitrary", "arbitrary"),
        name="pool_mixer",
    )(x3, hist, nw, w_in, w_grp, scale, w_out)


def _key_perm():
    n = np.arange(D_MODEL)
    kc, kl, h = n // LANES, (n % LANES) // N_HEADS, n % N_HEADS
    return h * HEAD_DIM + kc * SUBLANES + kl


def _value_rep_perm():
    n = np.arange(HEAD_DIM * LANES)
    return (n % N_HEADS) * HEAD_DIM + n // LANES


def _pad_rows(x3, rows):
    b, t, d = x3.shape
    return jnp.concatenate([x3, jnp.zeros((b, rows - t, d), x3.dtype)], axis=1)


def _state_to_tiles(s):
    b = s.shape[0]
    s = s.reshape(b, N_HEADS, HEAD_DIM, K_HI, SUBLANES)
    return s.transpose(0, 3, 2, 4, 1).reshape(b, K_HI, HEAD_DIM, LANES)


def _tiles_to_state(s):
    b = s.shape[0]
    s = s.reshape(b, K_HI, HEAD_DIM, SUBLANES, N_HEADS)
    return s.transpose(0, 4, 2, 1, 3).reshape(b, N_HEADS, HEAD_DIM, HEAD_DIM)


def _head_col_tile(vec):
    return jnp.tile(vec.reshape(N_HEADS, HEAD_DIM).T, (1, SUBLANES))


def _fox_layer(j, n_fox, stacked, y_p, y_s, nw, cache_kt, cache_vt, cache_lft, page_table,
               fox_w_in, fox_b_f, fox_q_gain, fox_k_gain, fox_w_out, tri):
    d, h = D_MODEL, N_HEADS
    bp, tp, _ = y_p.shape
    bs, ts, _ = y_s.shape
    w_in = fox_w_in[j]
    w_main = w_in[:, :4 * d].astype(BF16)
    w_f = jnp.pad(w_in[:, 4 * d:], ((0, 0), (0, LANES - h))).astype(BF16)
    b_f = jnp.pad(fox_b_f[j], (0, LANES - h)).reshape(1, LANES)
    qg = jnp.tile(fox_q_gain[j], h).reshape(1, d)
    kg = jnp.tile(fox_k_gain[j], h).reshape(1, d)
    gmat = jnp.asarray(np.kron(np.eye(h), np.ones((HEAD_DIM, HEAD_DIM))), dtype=BF16)
    w_out = fox_w_out[j].astype(BF16)

    xp = y_p.reshape(bp * tp, d)
    qb, kb, vb, g, kt_st, vt_st, lft_st = _fox_proj_prompt(
        j, n_fox, y_p, nw, w_main, w_f, b_f, qg, kg, gmat, stacked, tm=min(tp, 256))
    cum = _cumsum_time(lft_st[j], tri)
    sh = (bp, tp, d)
    o = _fox_attention(qb.reshape(sh), kb.reshape(sh), vb.reshape(sh), cum, tq=min(tp, 512))
    y_p_new = _gated_out(o.reshape(bp * tp, d), g, xp, w_out, tm=512).reshape(bp, tp, d)
    new_p = (kt_st, vt_st, lft_st)

    xs = y_s.reshape(bs * ts, d)
    qb, k, kb, v, vb, g, lf = _fox_proj(xs, nw, w_main, w_f, b_f, qg, kg, gmat, tm=bs * ts)
    lf_s = lf[:, :h].reshape(bs, ts, h)
    head_of_lane = np.arange(d) // HEAD_DIM
    head_mask = jnp.asarray(head_of_lane[None, :] == np.arange(h)[:, None])
    q_bd = jnp.where(head_mask[None, None], qb.reshape(bs, ts, 1, d), jnp.zeros((), BF16))
    q_bd = q_bd.reshape(bs, ts * h, d)
    as_page = lambda z: jnp.pad(z.reshape(bs, ts, d).transpose(0, 2, 1),
                                ((0, 0), (0, 0), (0, PAGE_SIZE - ts)))
    lfn = jnp.pad(lf_s.transpose(0, 2, 1), ((0, 0), (0, 0), (0, PAGE_SIZE - ts)))
    o = _fox_decode(j, page_table, q_bd, cache_kt, cache_vt, cache_lft, as_page(kb), as_page(vb),
                    lfn, tri)
    y_s_new = _gated_out(o.reshape(bs * ts, d), g, xs, w_out, tm=bs * ts).reshape(bs, ts, d)
    new_s = (k.reshape(bs, ts, h, HEAD_DIM), v.reshape(bs, ts, h, HEAD_DIM), lf_s)
    return y_p_new, y_s_new, new_p, new_s


def _rwkv_group(y3, x_last, s0_tiles, nw, prm, tm, t_valid, nb, tb):
    b, t, d = y3.shape
    (mu8, wr, wk, wv, wg, w1, w2, a1, a2, vecs, rk, lnw, lnb, w_out) = prm
    r, dec, k, kk, beta, v, g, last = _rwkv_pre(y3, x_last, nw, mu8, wr, wk, wv, wg, w1, w2, a1, a2,
                                                vecs, tm=tm, t_last=(t_valid - 1) % tm)
    m = b * t
    rowf = lambda z: z[:, :t_valid].reshape(b, t_valid, K_HI, LANES)
    z, s_fin = _wkv_scan(rowf(r), rowf(dec), rowf(k), rowf(kk), rowf(beta), rowf(v), s0_tiles,
                         lnw, lnb, rk, nb=nb, tb=tb)
    z = z.reshape(b, t_valid, d)
    if t_valid != t:
        z = _pad_rows(z, t)
    tmo = min(m, 512)
    y_new = _gated_out(z.reshape(m, d), g.reshape(m, d), y3.reshape(m, d), w_out, tm=tmo)
    return y_new.reshape(b, t, d), s_fin, last[:, 0]


def _rwkv_params(j, rwkv_mu, rwkv_w_rkvg, rwkv_w0, rwkv_w1, rwkv_w2, rwkv_a0, rwkv_a1, rwkv_a2,
                 rwkv_k_k, rwkv_k_a, rwkv_r_k, rwkv_ln_w, rwkv_ln_b, rwkv_w_out):
    d = D_MODEL
    perm = _key_perm()
    lora = rwkv_w1.shape[-1]
    mu8 = jnp.pad(rwkv_mu[j], ((0, 2), (0, 0)))
    wr = rwkv_w_rkvg[j, 0][:, perm].astype(BF16)
    wk = rwkv_w_rkvg[j, 1][:, perm].astype(BF16)
    wv = rwkv_w_rkvg[j, 2][:, perm].astype(BF16)
    wg = rwkv_w_rkvg[j, 3][:, perm].astype(BF16)
    w1 = jnp.pad(rwkv_w1[j], ((0, 0), (0, LANES - lora))).astype(BF16)
    w2 = jnp.pad(rwkv_w2[j][:, perm], ((0, LANES - lora), (0, 0))).astype(BF16)
    a1 = jnp.pad(rwkv_a1[j], ((0, 0), (0, LANES - lora))).astype(BF16)
    a2 = jnp.pad(rwkv_a2[j][:, perm], ((0, LANES - lora), (0, 0))).astype(BF16)
    vecs = jnp.stack([rwkv_w0[j][perm], rwkv_a0[j][perm], rwkv_k_k[j][perm], rwkv_k_a[j][perm]])
    vecs = jnp.pad(vecs, ((0, 4), (0, 0)))
    rk = rwkv_r_k[j].reshape(d)[perm].reshape(K_HI, LANES)
    lnw = _head_col_tile(rwkv_ln_w[j])
    lnb = _head_col_tile(rwkv_ln_b[j])
    w_out = rwkv_w_out[j][perm, :].astype(BF16)
    return (mu8, wr, wk, wv, wg, w1, w2, a1, a2, vecs, rk, lnw, lnb, w_out)


def kernel(x_prompt, x_sample, cache_k, cache_v, cache_logf, page_table, state_wkv, state_shift,
           state_pool, norm_w, fox_w_in, fox_b_f, fox_q_gain, fox_k_gain, fox_w_out,
           rwkv_mu, rwkv_w_rkvg, rwkv_w0, rwkv_w1, rwkv_w2, rwkv_a0, rwkv_a1, rwkv_a2,
           rwkv_k_k, rwkv_k_a, rwkv_r_k, rwkv_ln_w, rwkv_ln_b, rwkv_w_out,
           pool_w_in, pool_w_grp, pool_scale, pool_w_out):
    d = D_MODEL
    depth = norm_w.shape[0]
    bp, tp, _ = x_prompt.shape
    bs, ts, _ = x_sample.shape
    past = page_table.shape[1] * PAGE_SIZE
    n_fox, n_pool_pages = cache_k.shape[:2]
    cache_kt = cache_k.transpose(0, 1, 3, 4, 2)
    cache_vt = cache_v.transpose(0, 1, 3, 4, 2)
    cache_lft = cache_logf.transpose(0, 1, 3, 2)
    tri = _tri_mats()
    ts_pad = SUBLANES * pl.cdiv(ts, SUBLANES)

    y_p, y_s = x_prompt, x_sample
    ks, vs, ls = [], [], []
    wkv_p, sh_p, wkv_s, sh_s, pool_p, pool_s = [], [], [], [], [], []
    n_fox_layers = (depth + 2) // 3
    prompt_kv = None
    for i in range(depth):
        kind, j = i % 3, i // 3
        nw = norm_w[i].reshape(1, d)
        if kind == 0:
            y_p, y_s, prompt_kv, new_s = _fox_layer(
                j, n_fox_layers, prompt_kv, y_p, y_s, nw, cache_kt, cache_vt, cache_lft,
                page_table, fox_w_in, fox_b_f, fox_q_gain, fox_k_gain, fox_w_out, tri)
            ks.append(new_s[0]); vs.append(new_s[1]); ls.append(new_s[2])
        elif kind == 1:
            prm = _rwkv_params(j, rwkv_mu, rwkv_w_rkvg, rwkv_w0, rwkv_w1, rwkv_w2, rwkv_a0,
                               rwkv_a1, rwkv_a2, rwkv_k_k, rwkv_k_a, rwkv_r_k, rwkv_ln_w,
                               rwkv_ln_b, rwkv_w_out)
            zero_state = jnp.zeros((bp, K_HI, HEAD_DIM, LANES), F32)
            y_p, s_fin, last = _rwkv_group(y_p, jnp.zeros((bp, 1, d), F32), zero_state, nw, prm,
                                           tm=min(tp, 256), t_valid=tp, nb=bp, tb=min(tp, 64))
            wkv_p.append(_tiles_to_state(s_fin)); sh_p.append(last)
            y_s_pad, s_fin, last = _rwkv_group(
                _pad_rows(y_s, ts_pad), state_shift[j].reshape(bs, 1, d),
                _state_to_tiles(state_wkv[j]), nw, prm, tm=ts_pad, t_valid=ts, nb=2, tb=ts)
            y_s = y_s_pad[:, :ts]
            wkv_s.append(_tiles_to_state(s_fin)); sh_s.append(last)
        else:
            w_in = pool_w_in[j].astype(BF16)
            w_grp = pool_w_grp[j].astype(BF16)
            scale = pool_scale[j].reshape(1, d)
            w_out = pool_w_out[j].astype(BF16)
            y_p, tail = _pool_mixer(y_p, jnp.zeros((bp, POOL_HIST, d), F32), nw, w_in, w_grp,
                                    scale, w_out, tm=min(tp, 256), t_valid=min(tp, 256), pos0=0)
            pool_p.append(tail[:, 1:])
            hist = jnp.pad(state_pool[j], ((0, 0), (1, 0), (0, 0)))
            y_s_pad, tail = _pool_mixer(_pad_rows(y_s, ts_pad), hist, nw, w_in, w_grp, scale,
                                        w_out, tm=ts_pad, t_valid=ts, pos0=past)
            y_s = y_s_pad[:, :ts]
            pool_s.append(tail[:, 1:])
    st = lambda xs: jnp.stack(xs, 0)
    kt_st, vt_st, lft_st = prompt_kv
    to_thd = lambda z: z.transpose(0, 1, 4, 2, 3)
    return (y_p, y_s, to_thd(kt_st), to_thd(vt_st), lft_st.transpose(0, 1, 3, 2),
            st(ks), st(vs), st(ls),
            st(wkv_p), st(sh_p), st(wkv_s), st(sh_s), st(pool_p), st(pool_s))
```

```python
import functools

import jax
import jax.numpy as jnp
import numpy as np
from jax import lax
from jax.experimental import pallas as pl
from jax.experimental.pallas import tpu as pltpu

F32 = jnp.float32
BF16 = jnp.bfloat16

D_MODEL = 1024
HEAD_DIM = 64
N_HEADS = D_MODEL // HEAD_DIM
PAGE_SIZE = 128
POOL_WINDOWS = (2, 4, 8, 16)
POOL_GROUP = D_MODEL // len(POOL_WINDOWS)
POOL_HIST = 16
RMS_EPS = 1e-6
GN_EPS = 64e-5
L2_EPS_SQ = 1e-24
ATTN_SCALE = HEAD_DIM ** -0.5
LOG2E = 1.4426950408889634

LANES = 128
SUBLANES = 8
K_HI = HEAD_DIM // SUBLANES
VMEM_LIMIT = 56 * 1024 * 1024
DECODE_PAGES_PER_STEP = 16
ATTN_HEADS_PER_STEP = 2


def _cparams(*sem):
    return pltpu.CompilerParams(dimension_semantics=sem, vmem_limit_bytes=VMEM_LIMIT)


def _const_spec(shape):
    nd = len(shape)
    return pl.BlockSpec(shape, lambda *_: (0,) * nd, pipeline_mode=pl.Buffered(1))


def _rmsnorm_rows(x, w):
    ms = jnp.mean(x * x, axis=-1, keepdims=True)
    return x * lax.rsqrt(ms + RMS_EPS) * w


def _sigmoid(x):
    return 1.0 / (1.0 + jnp.exp(-x))


def _bdot(a, b):
    return jnp.dot(a.astype(BF16), b, preferred_element_type=F32)


def _dot3(x, m):
    hi = x.astype(BF16)
    r1 = x - hi.astype(F32)
    mid = r1.astype(BF16)
    lo = (r1 - mid.astype(F32)).astype(BF16)
    return (jnp.dot(hi, m, preferred_element_type=F32)
            + jnp.dot(mid, m, preferred_element_type=F32)
            + jnp.dot(lo, m, preferred_element_type=F32))


def _lane_group_sum(x):
    x = x + pltpu.roll(x, 16, axis=x.ndim - 1)
    x = x + pltpu.roll(x, 32, axis=x.ndim - 1)
    return x + pltpu.roll(x, 64, axis=x.ndim - 1)


def _fox_proj_kernel(x_ref, nw_ref, w_ref, wf_ref, bf_ref, qg_ref, kg_ref, gm_ref, *refs,
                     time_minor, own_slot=None):
    if time_minor:
        qb_ref, kb_ref, vb_ref, g_ref, kt_ref, vt_ref, lft_ref = refs[-7:]
    else:
        qb_ref, k_ref, kb_ref, v_ref, vb_ref, g_ref, lf_ref = refs
    xb = _rmsnorm_rows(x_ref[...], nw_ref[...]).astype(BF16)

    def headnorm(z, gain):
        ms = _bdot(z * z, gm_ref[...]) * (1.0 / HEAD_DIM)
        return z * lax.rsqrt(ms + RMS_EPS) * gain

    d = D_MODEL
    q = headnorm(jnp.dot(xb, w_ref[:, 0:d], preferred_element_type=F32), qg_ref[...])
    qb_ref[...] = (q * (ATTN_SCALE * LOG2E)).astype(BF16)
    k = headnorm(jnp.dot(xb, w_ref[:, d:2 * d], preferred_element_type=F32), kg_ref[...])
    kb_ref[...] = k.astype(BF16)
    v = jnp.dot(xb, w_ref[:, 2 * d:3 * d], preferred_element_type=F32)
    vb_ref[...] = v.astype(BF16)
    g_ref[...] = jnp.dot(xb, w_ref[:, 3 * d:4 * d], preferred_element_type=F32)
    fl = jnp.dot(xb, wf_ref[...], preferred_element_type=F32) + bf_ref[...]
    lf = -(jnp.maximum(-fl, 0.0) + jnp.log(1.0 + jnp.exp(-jnp.abs(fl))))
    if time_minor:
        tm = k.shape[0]
        if own_slot is not None:
            for ref in (kt_ref, vt_ref, lft_ref):
                ref[...] = jnp.zeros_like(ref)
        slot = 0 if own_slot is None else own_slot
        kt_ref[slot, 0] = k.T.reshape(N_HEADS, HEAD_DIM, tm)
        vt_ref[slot, 0] = v.T.reshape(N_HEADS, HEAD_DIM, tm)
        lft_ref[slot, 0] = lf.T[0:N_HEADS, :]
    else:
        k_ref[...] = k
        v_ref[...] = v
        lf_ref[...] = lf


def _fox_proj(x2d, nw, w_main, w_f, b_f, qg, kg, gmat, tm):
    m = x2d.shape[0]
    d = D_MODEL
    row = lambda n: pl.BlockSpec((tm, n), lambda i: (i, 0))
    outs = (jax.ShapeDtypeStruct((m, d), BF16), jax.ShapeDtypeStruct((m, d), F32),
            jax.ShapeDtypeStruct((m, d), BF16), jax.ShapeDtypeStruct((m, d), F32),
            jax.ShapeDtypeStruct((m, d), BF16), jax.ShapeDtypeStruct((m, d), F32),
            jax.ShapeDtypeStruct((m, LANES), F32))
    return pl.pallas_call(
        functools.partial(_fox_proj_kernel, time_minor=False),
        grid=(m // tm,),
        in_specs=[row(d), _const_spec((1, d)), _const_spec((d, 4 * d)), _const_spec((d, LANES)),
                  _const_spec((1, LANES)), _const_spec((1, d)), _const_spec((1, d)),
                  _const_spec((d, d))],
        out_specs=[row(d)] * 6 + [row(LANES)],
        out_shape=outs,
        compiler_params=_cparams("arbitrary"),
        name="fox_proj",
    )(x2d, nw, w_main, w_f, b_f, qg, kg, gmat)


def _fox_proj_prompt(layer, n_layers, x3, nw, w_main, w_f, b_f, qg, kg, gmat, stacked, tm):
    b, t, d = x3.shape
    nt = t // tm
    m = b * t
    row = pl.BlockSpec((tm, d), lambda i: (i, 0))
    own_all = stacked is None
    nl, l0 = (n_layers, 0) if own_all else (1, layer)
    kv_spec = pl.BlockSpec((nl, 1, N_HEADS, HEAD_DIM, tm), lambda i: (l0, i // nt, 0, 0, i % nt))
    lf_spec = pl.BlockSpec((nl, 1, N_HEADS, tm), lambda i: (l0, i // nt, 0, i % nt))
    kv_shape = jax.ShapeDtypeStruct((n_layers, b, N_HEADS, HEAD_DIM, t), F32)
    outs = (jax.ShapeDtypeStruct((m, d), BF16), jax.ShapeDtypeStruct((m, d), BF16),
            jax.ShapeDtypeStruct((m, d), BF16), jax.ShapeDtypeStruct((m, d), F32),
            kv_shape, kv_shape, jax.ShapeDtypeStruct((n_layers, b, N_HEADS, t), F32))
    in_specs = [row, _const_spec((1, d)), _const_spec((d, 4 * d)), _const_spec((d, LANES)),
                _const_spec((1, LANES)), _const_spec((1, d)), _const_spec((1, d)),
                _const_spec((d, d))]
    args = [x3.reshape(m, d), nw, w_main, w_f, b_f, qg, kg, gmat]
    aliases = {}
    if stacked is not None:
        in_specs += [pl.BlockSpec(memory_space=pl.ANY)] * 3
        aliases = {len(args) + i: 4 + i for i in range(3)}
        args += list(stacked)
    return pl.pallas_call(
        functools.partial(_fox_proj_kernel, time_minor=True, own_slot=layer if own_all else None),
        grid=(m // tm,),
        in_specs=in_specs,
        out_specs=[row] * 4 + [kv_spec, kv_spec, lf_spec],
        out_shape=outs,
        input_output_aliases=aliases,
        compiler_params=_cparams("arbitrary"),
        name="fox_proj_prompt",
    )(*args)


def _cumsum_kernel(x_ref, tri_ref, o_ref, carry_ref, *, nchunk):
    @pl.when(pl.program_id(1) == 0)
    def _():
        carry_ref[...] = jnp.zeros_like(carry_ref)

    carry = carry_ref[...]
    upper = tri_ref[0]
    ones = tri_ref[1]
    for c in range(nchunk):
        x = x_ref[0, :, c * LANES:(c + 1) * LANES]
        o_ref[0, :, c * LANES:(c + 1) * LANES] = (carry + _dot3(x, upper)) * LOG2E
        carry = carry + _dot3(x, ones)
    carry_ref[...] = carry


def _tri_mats():
    i = np.arange(LANES)
    upper = (i[:, None] <= i[None, :]).astype(np.float32)
    return jnp.asarray(np.stack([upper, np.ones_like(upper)]), dtype=BF16)


def _cumsum_time(x_bht, tri):
    b, h, t = x_bht.shape
    tc = min(t, 1024)
    return pl.pallas_call(
        functools.partial(_cumsum_kernel, nchunk=tc // LANES),
        grid=(b, t // tc),
        in_specs=[pl.BlockSpec((1, h, tc), lambda i, j: (i, 0, j)), _const_spec((2, LANES, LANES))],
        out_specs=pl.BlockSpec((1, h, tc), lambda i, j: (i, 0, j)),
        out_shape=jax.ShapeDtypeStruct((b, h, t), F32),
        scratch_shapes=[pltpu.VMEM((h, LANES), F32)],
        compiler_params=_cparams("arbitrary", "arbitrary"),
        name="logf_cumsum",
    )(x_bht, tri)


def _fox_attn_kernel(q_ref, k_ref, v_ref, c_ref, o_ref, s0_sc, s1_sc, m_sc, acc_sc, *, tq, nh):
    qi = pl.program_id(2)
    lane = lax.broadcasted_iota(jnp.int32, (tq, LANES), 1)
    halves = (lane < HEAD_DIM, lane >= HEAD_DIM)
    group = lambda ref_row, hh: ref_row[:, (hh // 2) * LANES:(hh // 2 + 1) * LANES]
    q_all = q_ref[0]
    qh = [jnp.where(halves[hh % 2], group(q_all, hh), jnp.zeros((), BF16)) for hh in range(nh)]

    def score(kj, dst):
        k_all = k_ref[0, pl.ds(pl.multiple_of(kj * tq, tq), tq), :]
        for hh in range(nh):
            dst[hh] = lax.dot_general(qh[hh], group(k_all, hh), (((1,), (1,)), ((), ())),
                                      preferred_element_type=F32)

    def consume(kj, src, diagonal):
        v_all = v_ref[0, pl.ds(pl.multiple_of(kj * tq, tq), tq), :]
        vh = [jnp.where(halves[hh % 2], group(v_all, hh), jnp.ones((), BF16)) for hh in range(nh)]
        cj = c_ref[0, 0, kj]
        for hh in range(nh):
            s = src[hh] - cj[hh:hh + 1, :]
            if diagonal:
                row = lax.broadcasted_iota(jnp.int32, (tq, tq), 0)
                col = lax.broadcasted_iota(jnp.int32, (tq, tq), 1)
                s = jnp.where(row >= col, s, -jnp.inf)
            part = s[:, 0:LANES]
            for c in range(1, tq // LANES):
                part = jnp.maximum(part, s[:, c * LANES:(c + 1) * LANES])
            m = m_sc[hh]
            m_new = jnp.maximum(m, jnp.max(part, axis=-1, keepdims=True))
            p = jnp.exp2(s - jnp.concatenate([m_new] * (tq // LANES), axis=1))
            acc_sc[hh] = jnp.exp2(m - m_new) * acc_sc[hh] + jnp.dot(
                p.astype(BF16), vh[hh], preferred_element_type=F32)
            m_sc[hh] = m_new

    m_sc[...] = jnp.full_like(m_sc, -jnp.inf)
    acc_sc[...] = jnp.zeros_like(acc_sc)
    score(0, s0_sc)

    def pair(pp, carry):
        j = 2 * pp
        score(j + 1, s1_sc)
        consume(j, s0_sc, False)
        score(j + 2, s0_sc)
        consume(j + 1, s1_sc, False)
        return carry

    lax.fori_loop(0, qi // 2, pair, 0)

    @pl.when(qi % 2 == 1)
    def _():
        score(qi, s1_sc)
        consume(qi - 1, s0_sc, False)
        consume(qi, s1_sc, True)

    @pl.when(qi % 2 == 0)
    def _():
        consume(qi, s0_sc, True)

    outs = []
    for g in range(nh // 2):
        a0, a1 = acc_sc[2 * g], acc_sc[2 * g + 1]
        outs.append(jnp.where(halves[0], a0 / pltpu.roll(a0, HEAD_DIM, axis=1),
                              a1 / pltpu.roll(a1, HEAD_DIM, axis=1)))
    o_ref[0] = jnp.concatenate(outs, axis=1)


def _fox_attention(qb, kb, vb, cum_bht, tq):
    b, t, d = qb.shape
    nh = ATTN_HEADS_PER_STEP
    hg = N_HEADS // nh
    w = nh * HEAD_DIM
    nt = t // tq
    cum5 = cum_bht.reshape(b, hg, nh, nt, tq).transpose(0, 1, 3, 2, 4)
    return pl.pallas_call(
        functools.partial(_fox_attn_kernel, tq=tq, nh=nh),
        grid=(b, hg, nt),
        in_specs=[pl.BlockSpec((1, tq, w), lambda i, h, q: (i, q, h)),
                  pl.BlockSpec((1, t, w), lambda i, h, q: (i, 0, h)),
                  pl.BlockSpec((1, t, w), lambda i, h, q: (i, 0, h)),
                  pl.BlockSpec((1, 1, nt, nh, tq), lambda i, h, q: (i, h, 0, 0, 0))],
        out_specs=pl.BlockSpec((1, tq, w), lambda i, h, q: (i, q, h)),
        out_shape=jax.ShapeDtypeStruct((b, t, d), F32),
        scratch_shapes=[pltpu.VMEM((nh, tq, tq), F32), pltpu.VMEM((nh, tq, tq), F32),
                        pltpu.VMEM((nh, tq, LANES), F32), pltpu.VMEM((nh, tq, LANES), F32)],
        compiler_params=_cparams("arbitrary", "arbitrary", "arbitrary"),
        name="fox_prompt_attention",
    )(qb, kb, vb, cum5)


def _fox_decode_kernel(pt_ref, q_ref, *refs, n_q, n_par):
    del pt_ref
    k_refs, v_refs, lf_refs = refs[:n_par], refs[n_par:2 * n_par], refs[2 * n_par:3 * n_par]
    kn_ref, vn_ref, lfn_ref, tri_ref, o_ref, m_sc, l_sc, acc_sc, carry_sc = refs[3 * n_par:]
    p = pl.program_id(1)
    rows = n_q * N_HEADS

    @pl.when(p == 0)
    def _():
        m_sc[...] = jnp.full_like(m_sc, -jnp.inf)
        l_sc[...] = jnp.zeros_like(l_sc)
        acc_sc[...] = jnp.zeros_like(acc_sc)
        carry_sc[...] = jnp.zeros_like(carry_sc)

    q = q_ref[0]

    def cumulate(lft):
        cum = carry_sc[...] + _dot3(lft, tri_ref[0])
        carry_sc[...] = carry_sc[...] + _dot3(lft, tri_ref[1])
        return cum * LOG2E

    def update(kts, vts, cums, ok):
        ss = []
        for kt, cum in zip(kts, cums):
            s = jnp.dot(q, kt, preferred_element_type=F32) - jnp.concatenate([cum] * n_q, axis=0)
            ss.append(s if ok is None else jnp.where(ok, s, -jnp.inf))
        m = m_sc[...]
        m_new = m
        for s in ss:
            m_new = jnp.maximum(m_new, jnp.max(s, axis=-1, keepdims=True))
        alpha = jnp.exp2(m - m_new)
        l = alpha * l_sc[...]
        acc = alpha * acc_sc[...]
        for s, vt in zip(ss, vts):
            pr = jnp.exp2(s - m_new)
            l = l + jnp.sum(pr, axis=-1, keepdims=True)
            acc = acc + lax.dot_general(pr.astype(BF16), vt, (((1,), (1,)), ((), ())),
                                        preferred_element_type=F32)
        l_sc[...] = l
        acc_sc[...] = acc
        m_sc[...] = m_new

    update([r[0, 0].reshape(D_MODEL, PAGE_SIZE).astype(BF16) for r in k_refs],
           [r[0, 0].reshape(D_MODEL, PAGE_SIZE).astype(BF16) for r in v_refs],
           [cumulate(r[0, 0]) for r in lf_refs], None)

    @pl.when(p == pl.num_programs(1) - 1)
    def _():
        r = lax.broadcasted_iota(jnp.int32, (rows, PAGE_SIZE), 0)
        j = lax.broadcasted_iota(jnp.int32, (rows, PAGE_SIZE), 1)
        update([kn_ref[0]], [vn_ref[0]], [cumulate(lfn_ref[0])], j <= r // N_HEADS)
        o = acc_sc[...] / l_sc[...]
        rr = lax.broadcasted_iota(jnp.int32, (rows, D_MODEL), 0)
        cc = lax.broadcasted_iota(jnp.int32, (rows, D_MODEL), 1)
        o = jnp.where(rr % N_HEADS == cc // HEAD_DIM, o, 0.0)
        o_ref[0] = jnp.concatenate(
            [jnp.sum(o[N_HEADS * t:N_HEADS * (t + 1)], axis=0, keepdims=True) for t in range(n_q)],
            axis=0)


def _fox_decode(layer, page_table, q_bd, cache_kt, cache_vt, cache_lft, knt, vnt, lfn, tri):
    b, rows, d = q_bd.shape
    n_q = rows // N_HEADS
    n_pages = page_table.shape[1]
    n_par = DECODE_PAGES_PER_STEP if n_pages % DECODE_PAGES_PER_STEP == 0 else 1
    page5 = lambda u: (lambda i, p, pt: (layer, pt[i, p * n_par + u], 0, 0, 0))
    page4 = lambda u: (lambda i, p, pt: (layer, pt[i, p * n_par + u], 0, 0))
    per_b = lambda i, p, pt: (i, 0, 0)
    kv_specs = [pl.BlockSpec((1, 1, N_HEADS, HEAD_DIM, PAGE_SIZE), page5(u)) for u in range(n_par)]
    lf_specs = [pl.BlockSpec((1, 1, N_HEADS, PAGE_SIZE), page4(u)) for u in range(n_par)]
    grid_spec = pltpu.PrefetchScalarGridSpec(
        num_scalar_prefetch=1,
        grid=(b, n_pages // n_par),
        in_specs=[pl.BlockSpec((1, rows, d), per_b)] + kv_specs + kv_specs + lf_specs + [
            pl.BlockSpec((1, d, PAGE_SIZE), per_b),
            pl.BlockSpec((1, d, PAGE_SIZE), per_b),
            pl.BlockSpec((1, N_HEADS, PAGE_SIZE), per_b),
            pl.BlockSpec((2, LANES, LANES), lambda i, p, pt: (0, 0, 0))],
        out_specs=pl.BlockSpec((1, n_q, d), per_b),
        scratch_shapes=[pltpu.VMEM((rows, 1), F32), pltpu.VMEM((rows, 1), F32),
                        pltpu.VMEM((rows, d), F32), pltpu.VMEM((N_HEADS, LANES), F32)])
    return pl.pallas_call(
        functools.partial(_fox_decode_kernel, n_q=n_q, n_par=n_par),
        grid_spec=grid_spec,
        out_shape=jax.ShapeDtypeStruct((b, n_q, d), F32),
        compiler_params=_cparams("arbitrary", "arbitrary"),
        name="fox_sample_attention",
    )(page_table, q_bd, *([cache_kt] * n_par), *([cache_vt] * n_par), *([cache_lft] * n_par),
      knt, vnt, lfn, tri)


def _gated_out_kernel(o_ref, g_ref, res_ref, w_ref, y_ref):
    g = g_ref[...]
    z = o_ref[...] * (g * _sigmoid(g))
    y_ref[...] = res_ref[...] + _bdot(z, w_ref[...])


def _gated_out(o2d, g2d, res2d, w_out, tm):
    m, d = o2d.shape
    row = pl.BlockSpec((tm, d), lambda i: (i, 0))
    return pl.pallas_call(
        _gated_out_kernel,
        grid=(m // tm,),
        in_specs=[row, row, row, _const_spec((d, d))],
        out_specs=row,
        out_shape=jax.ShapeDtypeStruct((m, d), F32),
        compiler_params=_cparams("arbitrary"),
        name="gated_out_proj",
    )(o2d, g2d, res2d, w_out)


def _rwkv_pre_kernel(x_ref, xlast_ref, nw_ref, mu_ref, wr_ref, wk_ref, wv_ref, wg_ref, w1_ref,
                     w2_ref, a1_ref, a2_ref, vec_ref,
                     r_ref, dec_ref, k_ref, kk_ref, beta_ref, v_ref, g_ref, last_ref,
                     carry_ref, *, tm, t_last):
    @pl.when(pl.program_id(1) == 0)
    def _():
        carry_ref[...] = xlast_ref[0]

    xn = _rmsnorm_rows(x_ref[0], nw_ref[...])
    rowi = lax.broadcasted_iota(jnp.int32, (tm, D_MODEL), 0)
    prev = jnp.where(rowi == 0, carry_ref[...], pltpu.roll(xn, 1, axis=0))
    carry_ref[...] = xn[tm - 1:tm, :]
    last_ref[0] = xn[t_last:t_last + 1, :]
    dx = prev - xn
    mix = lambda j: xn + dx * mu_ref[j:j + 1, :]

    w0, a0, k_k, k_a = (vec_ref[i:i + 1, :] for i in range(4))
    r = _bdot(mix(0), wr_ref[...])
    k = _bdot(mix(1), wk_ref[...])
    v_ref[0] = _bdot(mix(2), wv_ref[...])
    g_ref[0] = _bdot(mix(3), wg_ref[...])
    wl = w0 + _bdot(jnp.tanh(_bdot(mix(4), w1_ref[...])), w2_ref[...])
    w_log = -(jnp.maximum(-wl, 0.0) + jnp.log(1.0 + jnp.exp(-jnp.abs(wl)))) - 0.5
    dec_ref[0] = jnp.exp(-jnp.exp(w_log))
    a = _sigmoid(a0 + _bdot(_bdot(mix(5), a1_ref[...]), a2_ref[...]))
    kk = k * k_k
    sq = kk * kk
    tot = sq[:, 0:LANES]
    for c in range(1, K_HI):
        tot = tot + sq[:, c * LANES:(c + 1) * LANES]
    inv = lax.rsqrt(jnp.maximum(_lane_group_sum(tot), L2_EPS_SQ))
    kkn = kk * jnp.concatenate([inv] * K_HI, axis=1)
    r_ref[0] = r
    k_ref[0] = k * (1.0 + (a - 1.0) * k_a)
    kk_ref[0] = kkn
    beta_ref[0] = kkn * a


def _rwkv_pre(x3, x_last, nw, mu8, wr, wk, wv, wg, w1, w2, a1, a2, vecs, tm, t_last):
    b, t, d = x3.shape
    tile = pl.BlockSpec((1, tm, d), lambda i, j: (i, j, 0))
    per_b = pl.BlockSpec((1, 1, d), lambda i, j: (i, 0, 0))
    f32o = jax.ShapeDtypeStruct((b, t, d), F32)
    return pl.pallas_call(
        functools.partial(_rwkv_pre_kernel, tm=tm, t_last=t_last),
        grid=(b, t // tm),
        in_specs=[tile, per_b, _const_spec((1, d)), _const_spec((8, d)),
                  _const_spec((d, d)), _const_spec((d, d)), _const_spec((d, d)),
                  _const_spec((d, d)), _const_spec((d, LANES)), _const_spec((LANES, d)),
                  _const_spec((d, LANES)), _const_spec((LANES, d)), _const_spec((8, d))],
        out_specs=[tile] * 7 + [per_b],
        out_shape=(f32o,) * 7 + (jax.ShapeDtypeStruct((b, 1, d), F32),),
        scratch_shapes=[pltpu.VMEM((1, d), F32)],
        compiler_params=_cparams("arbitrary", "arbitrary"),
        name="rwkv_pre",
    )(x3, x_last, nw, mu8, wr, wk, wv, wg, w1, w2, a1, a2, vecs)


def _wkv_kernel(r_ref, dec_ref, k_ref, kk_ref, beta_ref, v_ref, s0_ref, lnw_ref, lnb_ref, rk_ref,
                gs_ref, z_ref, sf_ref, *scs, nb, tb):
    s_scs, sa_scs, ya_scs, vc_scs, pend_scs = (scs[i * nb:(i + 1) * nb] for i in range(5))
    own_group = (lax.broadcasted_iota(jnp.int32, (SUBLANES, LANES), 1) // N_HEADS
                 == lax.broadcasted_iota(jnp.int32, (SUBLANES, LANES), 0))

    def group_sum(x):
        hi = x.astype(BF16)
        lo = (x - hi.astype(F32)).astype(BF16)
        return (jnp.dot(hi, gs_ref[...], preferred_element_type=F32)
                + jnp.dot(lo, gs_ref[...], preferred_element_type=F32))

    def expand(b, t):
        rows = [jnp.where(own_group, jnp.broadcast_to(v_ref[b, t, m:m + 1, :], (SUBLANES, LANES)), 0.0)
                for m in range(K_HI)]
        return group_sum(jnp.concatenate(rows, axis=0))

    def compact(x):
        rows = [jnp.sum(jnp.where(own_group, x[SUBLANES * m:SUBLANES * (m + 1)], 0.0), axis=0,
                        keepdims=True) for m in range(K_HI)]
        return jnp.concatenate(rows, axis=0)

    @pl.when(pl.program_id(1) == 0)
    def _():
        for b in range(nb):
            s_scs[b][...] = s0_ref[b]

    def rows_total(x):
        return _lane_group_sum(jnp.sum(x, axis=0, keepdims=True))

    def tree_sum(xs):
        while len(xs) > 1:
            xs = [xs[i] + xs[i + 1] for i in range(0, len(xs), 2)]
        return xs[0]

    def reduce_sa(b, t):
        return -group_sum(
            tree_sum([s_scs[b][c] * kk_ref[b, t, c:c + 1, :] for c in range(K_HI)]))

    def reduce_y(b, t):
        wr = dec_ref[b, t] * r_ref[b, t]
        ya_scs[b][...] = tree_sum([s_scs[b][c] * wr[c:c + 1, :] for c in range(K_HI)])

    def update(b, t, sa):
        w = dec_ref[b, t]
        km = k_ref[b, t]
        beta = beta_ref[b, t]
        vc = expand(b, t)
        for c in range(K_HI):
            s_scs[b][c] = (s_scs[b][c] * w[c:c + 1, :] + sa * beta[c:c + 1, :]
                           + vc * km[c:c + 1, :])
        sa_scs[b][...] = sa
        vc_scs[b][...] = vc

    def tail(b, t):
        r = r_ref[b, t]
        km = k_ref[b, t]
        vc = vc_scs[b][...]
        y = (group_sum(ya_scs[b][...]) + sa_scs[b][...] * rows_total(beta_ref[b, t] * r)
             + vc * rows_total(km * r))
        mean = jnp.sum(y, axis=0, keepdims=True) * (1.0 / HEAD_DIM)
        dy = y - mean
        var = jnp.sum(dy * dy, axis=0, keepdims=True) * (1.0 / HEAD_DIM)
        yn = dy * lax.rsqrt(var + GN_EPS)
        bonus = rows_total(r * km * rk_ref[...])
        z_ref[b, t] = compact(yn * lnw_ref[...] + lnb_ref[...] + vc * bonus)

    def step(t, carry):
        sa0 = reduce_sa(0, t)
        for b in range(1, nb):
            update(b, t - 1, pend_scs[b][...])
        for b in range(nb):
            tail(b, t - 1)
        reduce_y(0, t)
        for b in range(1, nb):
            pend_scs[b][...] = reduce_sa(b, t)
        update(0, t, sa0)
        for b in range(1, nb):
            reduce_y(b, t)
        return carry

    sa0 = reduce_sa(0, 0)
    for b in range(1, nb):
        pend_scs[b][...] = reduce_sa(b, 0)
    for b in range(nb):
        reduce_y(b, 0)
    update(0, 0, sa0)
    lax.fori_loop(1, tb, step, 0, unroll=3)
    for b in range(1, nb):
        update(b, tb - 1, pend_scs[b][...])
    for b in range(nb):
        tail(b, tb - 1)

    @pl.when(pl.program_id(1) == pl.num_programs(1) - 1)
    def _():
        for b in range(nb):
            sf_ref[b] = s_scs[b][...]


def _wkv_scan(r4, dec4, k4, kk4, beta4, v4, s0, lnw, lnb, rk, nb, tb):
    b, t = r4.shape[:2]
    lane = np.arange(LANES)
    same_head = jnp.asarray(lane[:, None] % N_HEADS == lane[None, :] % N_HEADS, dtype=BF16)
    rowf = pl.BlockSpec((nb, tb, K_HI, LANES), lambda i, j: (i, j, 0, 0))
    st = pl.BlockSpec((nb, K_HI, HEAD_DIM, LANES), lambda i, j: (i, 0, 0, 0))
    return pl.pallas_call(
        functools.partial(_wkv_kernel, nb=nb, tb=tb),
        grid=(b // nb, t // tb),
        in_specs=[rowf] * 6 + [st, _const_spec((HEAD_DIM, LANES)), _const_spec((HEAD_DIM, LANES)),
                               _const_spec((K_HI, LANES)), _const_spec((LANES, LANES))],
        out_specs=[rowf, st],
        out_shape=(jax.ShapeDtypeStruct((b, t, K_HI, LANES), F32),
                   jax.ShapeDtypeStruct((b, K_HI, HEAD_DIM, LANES), F32)),
        scratch_shapes=([pltpu.VMEM((K_HI, HEAD_DIM, LANES), F32) for _ in range(nb)]
                        + [pltpu.VMEM((HEAD_DIM, LANES), F32) for _ in range(4 * nb)]),
        compiler_params=_cparams("arbitrary", "arbitrary"),
        name="wkv_scan",
    )(r4, dec4, k4, kk4, beta4, v4, s0, lnw, lnb, rk, same_head)


def _pool_kernel(x_ref, hist_ref, nw_ref, win_ref, wgrp_ref, scale_ref, wout_ref,
                 y_ref, tail_ref, ext_ref, *, tm, t_valid, pos0):
    j = pl.program_id(1)

    @pl.when(j == 0)
    def _():
        ext_ref[0:POOL_HIST, :] = hist_ref[0]

    x = x_ref[0]
    xb = _rmsnorm_rows(x, nw_ref[...]).astype(BF16)
    u = jnp.dot(xb, win_ref[:, 0:D_MODEL], preferred_element_type=F32)
    g = jnp.dot(xb, win_ref[:, D_MODEL:2 * D_MODEL], preferred_element_type=F32)
    ext_ref[POOL_HIST:POOL_HIST + tm, :] = u
    pos = pos0 + j * tm + lax.broadcasted_iota(jnp.int32, (tm, POOL_GROUP), 0)
    mixed = []
    for gi, w in enumerate(POOL_WINDOWS):
        lo, hi = gi * POOL_GROUP, (gi + 1) * POOL_GROUP
        win = u[:, lo:hi]
        for dlt in range(1, w):
            win = win + ext_ref[POOL_HIST - dlt:POOL_HIST - dlt + tm, lo:hi]
        cnt = jnp.minimum(pos + 1, w).astype(F32)
        diff = win / cnt - u[:, lo:hi]
        mixed.append(_bdot(diff, wgrp_ref[gi]))
    mix = jnp.concatenate(mixed, axis=1) * scale_ref[...]
    z = mix * (g * _sigmoid(g))
    y_ref[0] = x + _bdot(z, wout_ref[...])
    tail = ext_ref[t_valid:t_valid + POOL_HIST, :]
    tail_ref[0] = tail
    ext_ref[0:POOL_HIST, :] = tail


def _pool_mixer(x3, hist, nw, w_in, w_grp, scale, w_out, tm, t_valid, pos0):
    b, t, d = x3.shape
    tile = pl.BlockSpec((1, tm, d), lambda i, j: (i, j, 0))
    per_b = pl.BlockSpec((1, POOL_HIST, d), lambda i, j: (i, 0, 0))
    return pl.pallas_call(
        functools.partial(_pool_kernel, tm=tm, t_valid=t_valid, pos0=pos0),
        grid=(b, t // tm),
        in_specs=[tile, per_b, _const_spec((1, d)), _const_spec((d, 2 * d)),
                  _const_spec((len(POOL_WINDOWS), POOL_GROUP, POOL_GROUP)),
                  _const_spec((1, d)), _const_spec((d, d))],
        out_specs=[tile, per_b],
        out_shape=(jax.ShapeDtypeStruct((b, t, d), F32),
                   jax.ShapeDtypeStruct((b, POOL_HIST, d), F32)),
        scratch_shapes=[pltpu.VMEM((POOL_HIST + tm, d), F32)],
        compiler_params=_cparams("arbitrary", "arbitrary"),
        name="pool_mixer",
    )(x3, hist, nw, w_in, w_grp, scale, w_out)


def _key_perm():
    n = np.arange(D_MODEL)
    kc, kl, h = n // LANES, (n % LANES) // N_HEADS, n % N_HEADS
    return h * HEAD_DIM + kc * SUBLANES + kl


def _pad_rows(x3, rows):
    b, t, d = x3.shape
    return jnp.concatenate([x3, jnp.zeros((b, rows - t, d), x3.dtype)], axis=1)


def _state_to_tiles(s):
    b = s.shape[0]
    s = s.reshape(b, N_HEADS, HEAD_DIM, K_HI, SUBLANES)
    return s.transpose(0, 3, 2, 4, 1).reshape(b, K_HI, HEAD_DIM, LANES)


def _tiles_to_state(s):
    b = s.shape[0]
    s = s.reshape(b, K_HI, HEAD_DIM, SUBLANES, N_HEADS)
    return s.transpose(0, 4, 2, 1, 3).reshape(b, N_HEADS, HEAD_DIM, HEAD_DIM)


def _head_col_tile(vec):
    return jnp.tile(vec.reshape(N_HEADS, HEAD_DIM).T, (1, SUBLANES))


def _fox_layer(j, n_fox, stacked, y_p, y_s, nw, cache_kt, cache_vt, cache_lft, page_table,
               fox_w_in, fox_b_f, fox_q_gain, fox_k_gain, fox_w_out, tri):
    d, h = D_MODEL, N_HEADS
    bp, tp, _ = y_p.shape
    bs, ts, _ = y_s.shape
    w_in = fox_w_in[j]
    w_main = w_in[:, :4 * d].astype(BF16)
    w_f = jnp.pad(w_in[:, 4 * d:], ((0, 0), (0, LANES - h))).astype(BF16)
    b_f = jnp.pad(fox_b_f[j], (0, LANES - h)).reshape(1, LANES)
    qg = jnp.tile(fox_q_gain[j], h).reshape(1, d)
    kg = jnp.tile(fox_k_gain[j], h).reshape(1, d)
    gmat = jnp.asarray(np.kron(np.eye(h), np.ones((HEAD_DIM, HEAD_DIM))), dtype=BF16)
    w_out = fox_w_out[j].astype(BF16)

    xp = y_p.reshape(bp * tp, d)
    qb, kb, vb, g, kt_st, vt_st, lft_st = _fox_proj_prompt(
        j, n_fox, y_p, nw, w_main, w_f, b_f, qg, kg, gmat, stacked, tm=min(tp, 256))
    cum = _cumsum_time(lft_st[j], tri)
    sh = (bp, tp, d)
    o = _fox_attention(qb.reshape(sh), kb.reshape(sh), vb.reshape(sh), cum, tq=min(tp, 512))
    y_p_new = _gated_out(o.reshape(bp * tp, d), g, xp, w_out, tm=512).reshape(bp, tp, d)
    new_p = (kt_st, vt_st, lft_st)

    xs = y_s.reshape(bs * ts, d)
    qb, k, kb, v, vb, g, lf = _fox_proj(xs, nw, w_main, w_f, b_f, qg, kg, gmat, tm=bs * ts)
    lf_s = lf[:, :h].reshape(bs, ts, h)
    head_of_lane = np.arange(d) // HEAD_DIM
    head_mask = jnp.asarray(head_of_lane[None, :] == np.arange(h)[:, None])
    q_bd = jnp.where(head_mask[None, None], qb.reshape(bs, ts, 1, d), jnp.zeros((), BF16))
    q_bd = q_bd.reshape(bs, ts * h, d)
    as_page = lambda z: jnp.pad(z.reshape(bs, ts, d).transpose(0, 2, 1),
                                ((0, 0), (0, 0), (0, PAGE_SIZE - ts)))
    lfn = jnp.pad(lf_s.transpose(0, 2, 1), ((0, 0), (0, 0), (0, PAGE_SIZE - ts)))
    o = _fox_decode(j, page_table, q_bd, cache_kt, cache_vt, cache_lft, as_page(kb), as_page(vb),
                    lfn, tri)
    y_s_new = _gated_out(o.reshape(bs * ts, d), g, xs, w_out, tm=bs * ts).reshape(bs, ts, d)
    new_s = (k.reshape(bs, ts, h, HEAD_DIM), v.reshape(bs, ts, h, HEAD_DIM), lf_s)
    return y_p_new, y_s_new, new_p, new_s


def _rwkv_group(y3, x_last, s0_tiles, nw, prm, tm, t_valid, nb, tb):
    b, t, d = y3.shape
    (mu8, wr, wk, wv, wg, w1, w2, a1, a2, vecs, rk, lnw, lnb, w_out) = prm
    r, dec, k, kk, beta, v, g, last = _rwkv_pre(y3, x_last, nw, mu8, wr, wk, wv, wg, w1, w2, a1, a2,
                                                vecs, tm=tm, t_last=(t_valid - 1) % tm)
    m = b * t
    rowf = lambda z: z[:, :t_valid].reshape(b, t_valid, K_HI, LANES)
    z, s_fin = _wkv_scan(rowf(r), rowf(dec), rowf(k), rowf(kk), rowf(beta), rowf(v), s0_tiles,
                         lnw, lnb, rk, nb=nb, tb=tb)
    z = z.reshape(b, t_valid, d)
    if t_valid != t:
        z = _pad_rows(z, t)
    tmo = min(m, 512)
    y_new = _gated_out(z.reshape(m, d), g.reshape(m, d), y3.reshape(m, d), w_out, tm=tmo)
    return y_new.reshape(b, t, d), s_fin, last[:, 0]


def _rwkv_params(j, rwkv_mu, rwkv_w_rkvg, rwkv_w0, rwkv_w1, rwkv_w2, rwkv_a0, rwkv_a1, rwkv_a2,
                 rwkv_k_k, rwkv_k_a, rwkv_r_k, rwkv_ln_w, rwkv_ln_b, rwkv_w_out):
    d = D_MODEL
    perm = _key_perm()
    lora = rwkv_w1.shape[-1]
    mu8 = jnp.pad(rwkv_mu[j], ((0, 2), (0, 0)))
    wr = rwkv_w_rkvg[j, 0][:, perm].astype(BF16)
    wk = rwkv_w_rkvg[j, 1][:, perm].astype(BF16)
    wv = rwkv_w_rkvg[j, 2][:, perm].astype(BF16)
    wg = rwkv_w_rkvg[j, 3][:, perm].astype(BF16)
    w1 = jnp.pad(rwkv_w1[j], ((0, 0), (0, LANES - lora))).astype(BF16)
    w2 = jnp.pad(rwkv_w2[j][:, perm], ((0, LANES - lora), (0, 0))).astype(BF16)
    a1 = jnp.pad(rwkv_a1[j], ((0, 0), (0, LANES - lora))).astype(BF16)
    a2 = jnp.pad(rwkv_a2[j][:, perm], ((0, LANES - lora), (0, 0))).astype(BF16)
    vecs = jnp.stack([rwkv_w0[j][perm], rwkv_a0[j][perm], rwkv_k_k[j][perm], rwkv_k_a[j][perm]])
    vecs = jnp.pad(vecs, ((0, 4), (0, 0)))
    rk = rwkv_r_k[j].reshape(d)[perm].reshape(K_HI, LANES)
    lnw = _head_col_tile(rwkv_ln_w[j])
    lnb = _head_col_tile(rwkv_ln_b[j])
    w_out = rwkv_w_out[j][perm, :].astype(BF16)
    return (mu8, wr, wk, wv, wg, w1, w2, a1, a2, vecs, rk, lnw, lnb, w_out)


def kernel(x_prompt, x_sample, cache_k, cache_v, cache_logf, page_table, state_wkv, state_shift,
           state_pool, norm_w, fox_w_in, fox_b_f, fox_q_gain, fox_k_gain, fox_w_out,
           rwkv_mu, rwkv_w_rkvg, rwkv_w0, rwkv_w1, rwkv_w2, rwkv_a0, rwkv_a1, rwkv_a2,
           rwkv_k_k, rwkv_k_a, rwkv_r_k, rwkv_ln_w, rwkv_ln_b, rwkv_w_out,
           pool_w_in, pool_w_grp, pool_scale, pool_w_out):
    d = D_MODEL
    depth = norm_w.shape[0]
    bp, tp, _ = x_prompt.shape
    bs, ts, _ = x_sample.shape
    past = page_table.shape[1] * PAGE_SIZE
    n_fox, n_pool_pages = cache_k.shape[:2]
    cache_kt = cache_k.transpose(0, 1, 3, 4, 2)
    cache_vt = cache_v.transpose(0, 1, 3, 4, 2)
    cache_lft = cache_logf.transpose(0, 1, 3, 2)
    tri = _tri_mats()
    ts_pad = SUBLANES * pl.cdiv(ts, SUBLANES)

    y_p, y_s = x_prompt, x_sample
    ks, vs, ls = [], [], []
    wkv_p, sh_p, wkv_s, sh_s, pool_p, pool_s = [], [], [], [], [], []
    n_fox_layers = (depth + 2) // 3
    prompt_kv = None
    for i in range(depth):
        kind, j = i % 3, i // 3
        nw = norm_w[i].reshape(1, d)
        if kind == 0:
            y_p, y_s, prompt_kv, new_s = _fox_layer(
                j, n_fox_layers, prompt_kv, y_p, y_s, nw, cache_kt, cache_vt, cache_lft,
                page_table, fox_w_in, fox_b_f, fox_q_gain, fox_k_gain, fox_w_out, tri)
            ks.append(new_s[0]); vs.append(new_s[1]); ls.append(new_s[2])
        elif kind == 1:
            prm = _rwkv_params(j, rwkv_mu, rwkv_w_rkvg, rwkv_w0, rwkv_w1, rwkv_w2, rwkv_a0,
                               rwkv_a1, rwkv_a2, rwkv_k_k, rwkv_k_a, rwkv_r_k, rwkv_ln_w,
                               rwkv_ln_b, rwkv_w_out)
            zero_state = jnp.zeros((bp, K_HI, HEAD_DIM, LANES), F32)
            y_p, s_fin, last = _rwkv_group(y_p, jnp.zeros((bp, 1, d), F32), zero_state, nw, prm,
                                           tm=min(tp, 256), t_valid=tp, nb=bp, tb=min(tp, 128))
            wkv_p.append(_tiles_to_state(s_fin)); sh_p.append(last)
            y_s_pad, s_fin, last = _rwkv_group(
                _pad_rows(y_s, ts_pad), state_shift[j].reshape(bs, 1, d),
                _state_to_tiles(state_wkv[j]), nw, prm, tm=ts_pad, t_valid=ts, nb=2, tb=ts)
            y_s = y_s_pad[:, :ts]
            wkv_s.append(_tiles_to_state(s_fin)); sh_s.append(last)
        else:
            w_in = pool_w_in[j].astype(BF16)
            w_grp = pool_w_grp[j].astype(BF16)
            scale = pool_scale[j].reshape(1, d)
            w_out = pool_w_out[j].astype(BF16)
            y_p, tail = _pool_mixer(y_p, jnp.zeros((bp, POOL_HIST, d), F32), nw, w_in, w_grp,
                                    scale, w_out, tm=min(tp, 256), t_valid=min(tp, 256), pos0=0)
            pool_p.append(tail[:, 1:])
            hist = jnp.pad(state_pool[j], ((0, 0), (1, 0), (0, 0)))
            y_s_pad, tail = _pool_mixer(_pad_rows(y_s, ts_pad), hist, nw, w_in, w_grp, scale,
                                        w_out, tm=ts_pad, t_valid=ts, pos0=past)
            y_s = y_s_pad[:, :ts]
            pool_s.append(tail[:, 1:])
    st = lambda xs: jnp.stack(xs, 0)
    kt_st, vt_st, lft_st = prompt_kv
    to_thd = lambda z: z.transpose(0, 1, 4, 2, 3)
    return (y_p, y_s, to_thd(kt_st), to_thd(vt_st), lft_st.transpose(0, 1, 3, 2),
            st(ks), st(vs), st(ls),
            st(wkv_p), st(sh_p), st(wkv_s), st(sh_s), st(pool_p), st(pool_s))
```

```python
import functools

import jax
import jax.numpy as jnp
import numpy as np
from jax import lax
from jax.experimental import pallas as pl
from jax.experimental.pallas import tpu as pltpu

F32 = jnp.float32
BF16 = jnp.bfloat16

D_MODEL = 1024
HEAD_DIM = 64
N_HEADS = D_MODEL // HEAD_DIM
PAGE_SIZE = 128
POOL_WINDOWS = (2, 4, 8, 16)
POOL_GROUP = D_MODEL // len(POOL_WINDOWS)
POOL_HIST = 16
RMS_EPS = 1e-6
GN_EPS = 64e-5
L2_EPS_SQ = 1e-24
ATTN_SCALE = HEAD_DIM ** -0.5
LOG2E = 1.4426950408889634

LANES = 128
SUBLANES = 8
K_HI = HEAD_DIM // SUBLANES
VMEM_LIMIT = 56 * 1024 * 1024
DECODE_PAGES_PER_STEP = 16
ATTN_HEADS_PER_STEP = 4


def _cparams(*sem):
    return pltpu.CompilerParams(dimension_semantics=sem, vmem_limit_bytes=VMEM_LIMIT)


def _const_spec(shape):
    nd = len(shape)
    return pl.BlockSpec(shape, lambda *_: (0,) * nd, pipeline_mode=pl.Buffered(1))


def _rmsnorm_rows(x, w):
    ms = jnp.mean(x * x, axis=-1, keepdims=True)
    return x * lax.rsqrt(ms + RMS_EPS) * w


def _sigmoid(x):
    return 1.0 / (1.0 + jnp.exp(-x))


def _bdot(a, b):
    return jnp.dot(a.astype(BF16), b, preferred_element_type=F32)


def _dot3(x, m):
    hi = x.astype(BF16)
    r1 = x - hi.astype(F32)
    mid = r1.astype(BF16)
    lo = (r1 - mid.astype(F32)).astype(BF16)
    return (jnp.dot(hi, m, preferred_element_type=F32)
            + jnp.dot(mid, m, preferred_element_type=F32)
            + jnp.dot(lo, m, preferred_element_type=F32))


def _lane_group_sum(x):
    x = x + pltpu.roll(x, 16, axis=x.ndim - 1)
    x = x + pltpu.roll(x, 32, axis=x.ndim - 1)
    return x + pltpu.roll(x, 64, axis=x.ndim - 1)


def _fox_proj_kernel(x_ref, nw_ref, w_ref, wf_ref, bf_ref, qg_ref, kg_ref, gm_ref, *refs,
                     time_minor, own_slot=None):
    if time_minor:
        qb_ref, kb_ref, vb_ref, g_ref, kt_ref, vt_ref, lft_ref = refs[-7:]
    else:
        qb_ref, k_ref, kb_ref, v_ref, vb_ref, g_ref, lf_ref = refs
    xb = _rmsnorm_rows(x_ref[...], nw_ref[...]).astype(BF16)

    def headnorm(z, gain):
        ms = _bdot(z * z, gm_ref[...]) * (1.0 / HEAD_DIM)
        return z * lax.rsqrt(ms + RMS_EPS) * gain

    d = D_MODEL
    q = headnorm(jnp.dot(xb, w_ref[:, 0:d], preferred_element_type=F32), qg_ref[...])
    qb_ref[...] = (q * (ATTN_SCALE * LOG2E)).astype(BF16)
    k = headnorm(jnp.dot(xb, w_ref[:, d:2 * d], preferred_element_type=F32), kg_ref[...])
    kb_ref[...] = k.astype(BF16)
    v = jnp.dot(xb, w_ref[:, 2 * d:3 * d], preferred_element_type=F32)
    vb_ref[...] = v.astype(BF16)
    g_ref[...] = jnp.dot(xb, w_ref[:, 3 * d:4 * d], preferred_element_type=F32)
    fl = jnp.dot(xb, wf_ref[...], preferred_element_type=F32) + bf_ref[...]
    lf = -(jnp.maximum(-fl, 0.0) + jnp.log(1.0 + jnp.exp(-jnp.abs(fl))))
    if time_minor:
        tm = k.shape[0]
        if own_slot is not None:
            for ref in (kt_ref, vt_ref, lft_ref):
                ref[...] = jnp.zeros_like(ref)
        slot = 0 if own_slot is None else own_slot
        kt_ref[slot, 0] = k.T.reshape(N_HEADS, HEAD_DIM, tm)
        vt_ref[slot, 0] = v.T.reshape(N_HEADS, HEAD_DIM, tm)
        lft_ref[slot, 0] = lf.T[0:N_HEADS, :]
    else:
        k_ref[...] = k
        v_ref[...] = v
        lf_ref[...] = lf


def _fox_proj(x2d, nw, w_main, w_f, b_f, qg, kg, gmat, tm):
    m = x2d.shape[0]
    d = D_MODEL
    row = lambda n: pl.BlockSpec((tm, n), lambda i: (i, 0))
    outs = (jax.ShapeDtypeStruct((m, d), BF16), jax.ShapeDtypeStruct((m, d), F32),
            jax.ShapeDtypeStruct((m, d), BF16), jax.ShapeDtypeStruct((m, d), F32),
            jax.ShapeDtypeStruct((m, d), BF16), jax.ShapeDtypeStruct((m, d), F32),
            jax.ShapeDtypeStruct((m, LANES), F32))
    return pl.pallas_call(
        functools.partial(_fox_proj_kernel, time_minor=False),
        grid=(m // tm,),
        in_specs=[row(d), _const_spec((1, d)), _const_spec((d, 4 * d)), _const_spec((d, LANES)),
                  _const_spec((1, LANES)), _const_spec((1, d)), _const_spec((1, d)),
                  _const_spec((d, d))],
        out_specs=[row(d)] * 6 + [row(LANES)],
        out_shape=outs,
        compiler_params=_cparams("arbitrary"),
        name="fox_proj",
    )(x2d, nw, w_main, w_f, b_f, qg, kg, gmat)


def _fox_proj_prompt(layer, n_layers, x3, nw, w_main, w_f, b_f, qg, kg, gmat, stacked, tm):
    b, t, d = x3.shape
    nt = t // tm
    m = b * t
    row = pl.BlockSpec((tm, d), lambda i: (i, 0))
    own_all = stacked is None
    nl, l0 = (n_layers, 0) if own_all else (1, layer)
    kv_spec = pl.BlockSpec((nl, 1, N_HEADS, HEAD_DIM, tm), lambda i: (l0, i // nt, 0, 0, i % nt))
    lf_spec = pl.BlockSpec((nl, 1, N_HEADS, tm), lambda i: (l0, i // nt, 0, i % nt))
    kv_shape = jax.ShapeDtypeStruct((n_layers, b, N_HEADS, HEAD_DIM, t), F32)
    outs = (jax.ShapeDtypeStruct((m, d), BF16), jax.ShapeDtypeStruct((m, d), BF16),
            jax.ShapeDtypeStruct((m, d), BF16), jax.ShapeDtypeStruct((m, d), F32),
            kv_shape, kv_shape, jax.ShapeDtypeStruct((n_layers, b, N_HEADS, t), F32))
    in_specs = [row, _const_spec((1, d)), _const_spec((d, 4 * d)), _const_spec((d, LANES)),
                _const_spec((1, LANES)), _const_spec((1, d)), _const_spec((1, d)),
                _const_spec((d, d))]
    args = [x3.reshape(m, d), nw, w_main, w_f, b_f, qg, kg, gmat]
    aliases = {}
    if stacked is not None:
        in_specs += [pl.BlockSpec(memory_space=pl.ANY)] * 3
        aliases = {len(args) + i: 4 + i for i in range(3)}
        args += list(stacked)
    return pl.pallas_call(
        functools.partial(_fox_proj_kernel, time_minor=True, own_slot=layer if own_all else None),
        grid=(m // tm,),
        in_specs=in_specs,
        out_specs=[row] * 4 + [kv_spec, kv_spec, lf_spec],
        out_shape=outs,
        input_output_aliases=aliases,
        compiler_params=_cparams("arbitrary"),
        name="fox_proj_prompt",
    )(*args)


def _cumsum_kernel(x_ref, tri_ref, o_ref, carry_ref, *, nchunk):
    @pl.when(pl.program_id(1) == 0)
    def _():
        carry_ref[...] = jnp.zeros_like(carry_ref)

    carry = carry_ref[...]
    upper = tri_ref[0]
    ones = tri_ref[1]
    for c in range(nchunk):
        x = x_ref[0, :, c * LANES:(c + 1) * LANES]
        o_ref[0, :, c * LANES:(c + 1) * LANES] = (carry + _dot3(x, upper)) * LOG2E
        carry = carry + _dot3(x, ones)
    carry_ref[...] = carry


def _tri_mats():
    i = np.arange(LANES)
    upper = (i[:, None] <= i[None, :]).astype(np.float32)
    return jnp.asarray(np.stack([upper, np.ones_like(upper)]), dtype=BF16)


def _cumsum_time(x_bht, tri):
    b, h, t = x_bht.shape
    tc = min(t, 1024)
    return pl.pallas_call(
        functools.partial(_cumsum_kernel, nchunk=tc // LANES),
        grid=(b, t // tc),
        in_specs=[pl.BlockSpec((1, h, tc), lambda i, j: (i, 0, j)), _const_spec((2, LANES, LANES))],
        out_specs=pl.BlockSpec((1, h, tc), lambda i, j: (i, 0, j)),
        out_shape=jax.ShapeDtypeStruct((b, h, t), F32),
        scratch_shapes=[pltpu.VMEM((h, LANES), F32)],
        compiler_params=_cparams("arbitrary", "arbitrary"),
        name="logf_cumsum",
    )(x_bht, tri)


def _fox_attn_kernel(q_ref, k_ref, v_ref, c_ref, o_ref, s0_sc, s1_sc, m_sc, acc_sc, *, tq, nh):
    qi = pl.program_id(2)
    lane = lax.broadcasted_iota(jnp.int32, (tq, LANES), 1)
    halves = (lane < HEAD_DIM, lane >= HEAD_DIM)
    group = lambda ref_row, hh: ref_row[:, (hh // 2) * LANES:(hh // 2 + 1) * LANES]
    q_all = q_ref[0]
    qh = [jnp.where(halves[hh % 2], group(q_all, hh), jnp.zeros((), BF16)) for hh in range(nh)]

    def score(kj, dst, heads):
        k_all = k_ref[0, pl.ds(pl.multiple_of(kj * tq, tq), tq), :]
        for hh in heads:
            dst[hh] = lax.dot_general(qh[hh], group(k_all, hh), (((1,), (1,)), ((), ())),
                                      preferred_element_type=F32)

    def consume(kj, src, diagonal, heads):
        v_all = v_ref[0, pl.ds(pl.multiple_of(kj * tq, tq), tq), :]
        vh = {hh: jnp.where(halves[hh % 2], group(v_all, hh), jnp.ones((), BF16)) for hh in heads}
        cj = c_ref[0, 0, kj]
        for hh in heads:
            s = src[hh] - cj[hh:hh + 1, :]
            if diagonal:
                row = lax.broadcasted_iota(jnp.int32, (tq, tq), 0)
                col = lax.broadcasted_iota(jnp.int32, (tq, tq), 1)
                s = jnp.where(row >= col, s, -jnp.inf)
            part = s[:, 0:LANES]
            for c in range(1, tq // LANES):
                part = jnp.maximum(part, s[:, c * LANES:(c + 1) * LANES])
            m = m_sc[hh]
            m_new = jnp.maximum(m, jnp.max(part, axis=-1, keepdims=True))
            p = jnp.exp2(s - jnp.concatenate([m_new] * (tq // LANES), axis=1))
            acc_sc[hh] = jnp.exp2(m - m_new) * acc_sc[hh] + jnp.dot(
                p.astype(BF16), vh[hh], preferred_element_type=F32)
            m_sc[hh] = m_new

    m_sc[...] = jnp.full_like(m_sc, -jnp.inf)
    acc_sc[...] = jnp.zeros_like(acc_sc)
    ha = tuple(range(0, nh, 2))
    hb = tuple(range(1, nh, 2))
    score(0, s0_sc, ha)

    def pair(pp, carry):
        j = 2 * pp
        score(j, s0_sc, hb)
        consume(j, s0_sc, False, ha)
        score(j + 1, s1_sc, ha)
        consume(j, s0_sc, False, hb)
        score(j + 1, s1_sc, hb)
        consume(j + 1, s1_sc, False, ha)
        score(j + 2, s0_sc, ha)
        consume(j + 1, s1_sc, False, hb)
        return carry

    lax.fori_loop(0, qi // 2, pair, 0)

    @pl.when(qi % 2 == 1)
    def _():
        score(qi - 1, s0_sc, hb)
        consume(qi - 1, s0_sc, False, ha)
        score(qi, s1_sc, ha)
        consume(qi - 1, s0_sc, False, hb)
        score(qi, s1_sc, hb)
        consume(qi, s1_sc, True, ha)
        consume(qi, s1_sc, True, hb)

    @pl.when(qi % 2 == 0)
    def _():
        score(qi, s0_sc, hb)
        consume(qi, s0_sc, True, ha)
        consume(qi, s0_sc, True, hb)

    outs = []
    for g in range(nh // 2):
        a0, a1 = acc_sc[2 * g], acc_sc[2 * g + 1]
        outs.append(jnp.where(halves[0], a0 / pltpu.roll(a0, HEAD_DIM, axis=1),
                              a1 / pltpu.roll(a1, HEAD_DIM, axis=1)))
    o_ref[0] = jnp.concatenate(outs, axis=1)


def _fox_attention(qb, kb, vb, cum_bht, tq):
    b, t, d = qb.shape
    nh = ATTN_HEADS_PER_STEP
    hg = N_HEADS // nh
    w = nh * HEAD_DIM
    nt = t // tq
    cum5 = cum_bht.reshape(b, hg, nh, nt, tq).transpose(0, 1, 3, 2, 4)
    return pl.pallas_call(
        functools.partial(_fox_attn_kernel, tq=tq, nh=nh),
        grid=(b, hg, nt),
        in_specs=[pl.BlockSpec((1, tq, w), lambda i, h, q: (i, q, h)),
                  pl.BlockSpec((1, t, w), lambda i, h, q: (i, 0, h)),
                  pl.BlockSpec((1, t, w), lambda i, h, q: (i, 0, h)),
                  pl.BlockSpec((1, 1, nt, nh, tq), lambda i, h, q: (i, h, 0, 0, 0))],
        out_specs=pl.BlockSpec((1, tq, w), lambda i, h, q: (i, q, h)),
        out_shape=jax.ShapeDtypeStruct((b, t, d), F32),
        scratch_shapes=[pltpu.VMEM((nh, tq, tq), F32), pltpu.VMEM((nh, tq, tq), F32),
                        pltpu.VMEM((nh, tq, LANES), F32), pltpu.VMEM((nh, tq, LANES), F32)],
        compiler_params=_cparams("arbitrary", "arbitrary", "arbitrary"),
        name="fox_prompt_attention",
    )(qb, kb, vb, cum5)


def _fox_decode_kernel(pt_ref, q_ref, *refs, n_q, n_par):
    del pt_ref
    k_refs, v_refs, lf_refs = refs[:n_par], refs[n_par:2 * n_par], refs[2 * n_par:3 * n_par]
    kn_ref, vn_ref, lfn_ref, tri_ref, o_ref, m_sc, l_sc, acc_sc, carry_sc = refs[3 * n_par:]
    p = pl.program_id(1)
    rows = n_q * N_HEADS

    @pl.when(p == 0)
    def _():
        m_sc[...] = jnp.full_like(m_sc, -jnp.inf)
        l_sc[...] = jnp.zeros_like(l_sc)
        acc_sc[...] = jnp.zeros_like(acc_sc)
        carry_sc[...] = jnp.zeros_like(carry_sc)

    q = q_ref[0]

    def cumulate(lft):
        cum = carry_sc[...] + _dot3(lft, tri_ref[0])
        carry_sc[...] = carry_sc[...] + _dot3(lft, tri_ref[1])
        return cum * LOG2E

    def update(kts, vts, cums, ok):
        ss = []
        for kt, cum in zip(kts, cums):
            s = jnp.dot(q, kt, preferred_element_type=F32) - jnp.concatenate([cum] * n_q, axis=0)
            ss.append(s if ok is None else jnp.where(ok, s, -jnp.inf))
        m = m_sc[...]
        m_new = m
        for s in ss:
            m_new = jnp.maximum(m_new, jnp.max(s, axis=-1, keepdims=True))
        alpha = jnp.exp2(m - m_new)
        l = alpha * l_sc[...]
        acc = alpha * acc_sc[...]
        for s, vt in zip(ss, vts):
            pr = jnp.exp2(s - m_new)
            l = l + jnp.sum(pr, axis=-1, keepdims=True)
            acc = acc + lax.dot_general(pr.astype(BF16), vt, (((1,), (1,)), ((), ())),
                                        preferred_element_type=F32)
        l_sc[...] = l
        acc_sc[...] = acc
        m_sc[...] = m_new

    update([r[0, 0].reshape(D_MODEL, PAGE_SIZE).astype(BF16) for r in k_refs],
           [r[0, 0].reshape(D_MODEL, PAGE_SIZE).astype(BF16) for r in v_refs],
           [cumulate(r[0, 0]) for r in lf_refs], None)

    @pl.when(p == pl.num_programs(1) - 1)
    def _():
        r = lax.broadcasted_iota(jnp.int32, (rows, PAGE_SIZE), 0)
        j = lax.broadcasted_iota(jnp.int32, (rows, PAGE_SIZE), 1)
        update([kn_ref[0]], [vn_ref[0]], [cumulate(lfn_ref[0])], j <= r // N_HEADS)
        o = acc_sc[...] / l_sc[...]
        rr = lax.broadcasted_iota(jnp.int32, (rows, D_MODEL), 0)
        cc = lax.broadcasted_iota(jnp.int32, (rows, D_MODEL), 1)
        o = jnp.where(rr % N_HEADS == cc // HEAD_DIM, o, 0.0)
        o_ref[0] = jnp.concatenate(
            [jnp.sum(o[N_HEADS * t:N_HEADS * (t + 1)], axis=0, keepdims=True) for t in range(n_q)],
            axis=0)


def _fox_decode(layer, page_table, q_bd, cache_kt, cache_vt, cache_lft, knt, vnt, lfn, tri):
    b, rows, d = q_bd.shape
    n_q = rows // N_HEADS
    n_pages = page_table.shape[1]
    n_par = DECODE_PAGES_PER_STEP if n_pages % DECODE_PAGES_PER_STEP == 0 else 1
    page5 = lambda u: (lambda i, p, pt: (layer, pt[i, p * n_par + u], 0, 0, 0))
    page4 = lambda u: (lambda i, p, pt: (layer, pt[i, p * n_par + u], 0, 0))
    per_b = lambda i, p, pt: (i, 0, 0)
    kv_specs = [pl.BlockSpec((1, 1, N_HEADS, HEAD_DIM, PAGE_SIZE), page5(u)) for u in range(n_par)]
    lf_specs = [pl.BlockSpec((1, 1, N_HEADS, PAGE_SIZE), page4(u)) for u in range(n_par)]
    grid_spec = pltpu.PrefetchScalarGridSpec(
        num_scalar_prefetch=1,
        grid=(b, n_pages // n_par),
        in_specs=[pl.BlockSpec((1, rows, d), per_b)] + kv_specs + kv_specs + lf_specs + [
            pl.BlockSpec((1, d, PAGE_SIZE), per_b),
            pl.BlockSpec((1, d, PAGE_SIZE), per_b),
            pl.BlockSpec((1, N_HEADS, PAGE_SIZE), per_b),
            pl.BlockSpec((2, LANES, LANES), lambda i, p, pt: (0, 0, 0))],
        out_specs=pl.BlockSpec((1, n_q, d), per_b),
        scratch_shapes=[pltpu.VMEM((rows, 1), F32), pltpu.VMEM((rows, 1), F32),
                        pltpu.VMEM((rows, d), F32), pltpu.VMEM((N_HEADS, LANES), F32)])
    return pl.pallas_call(
        functools.partial(_fox_decode_kernel, n_q=n_q, n_par=n_par),
        grid_spec=grid_spec,
        out_shape=jax.ShapeDtypeStruct((b, n_q, d), F32),
        compiler_params=_cparams("arbitrary", "arbitrary"),
        name="fox_sample_attention",
    )(page_table, q_bd, *([cache_kt] * n_par), *([cache_vt] * n_par), *([cache_lft] * n_par),
      knt, vnt, lfn, tri)


def _gated_out_kernel(o_ref, g_ref, res_ref, w_ref, y_ref):
    g = g_ref[...]
    z = o_ref[...] * (g * _sigmoid(g))
    y_ref[...] = res_ref[...] + _bdot(z, w_ref[...])


def _gated_out(o2d, g2d, res2d, w_out, tm):
    m, d = o2d.shape
    row = pl.BlockSpec((tm, d), lambda i: (i, 0))
    return pl.pallas_call(
        _gated_out_kernel,
        grid=(m // tm,),
        in_specs=[row, row, row, _const_spec((d, d))],
        out_specs=row,
        out_shape=jax.ShapeDtypeStruct((m, d), F32),
        compiler_params=_cparams("arbitrary"),
        name="gated_out_proj",
    )(o2d, g2d, res2d, w_out)


def _rwkv_pre_kernel(x_ref, xlast_ref, nw_ref, mu_ref, wr_ref, wk_ref, wv_ref, wg_ref, w1_ref,
                     w2_ref, a1_ref, a2_ref, vec_ref,
                     r_ref, dec_ref, k_ref, kk_ref, beta_ref, v_ref, g_ref, last_ref,
                     carry_ref, *, tm, t_last):
    @pl.when(pl.program_id(1) == 0)
    def _():
        carry_ref[...] = xlast_ref[0]

    xn = _rmsnorm_rows(x_ref[0], nw_ref[...])
    rowi = lax.broadcasted_iota(jnp.int32, (tm, D_MODEL), 0)
    prev = jnp.where(rowi == 0, carry_ref[...], pltpu.roll(xn, 1, axis=0))
    carry_ref[...] = xn[tm - 1:tm, :]
    last_ref[0] = xn[t_last:t_last + 1, :]
    dx = prev - xn
    mix = lambda j: xn + dx * mu_ref[j:j + 1, :]

    w0, a0, k_k, k_a = (vec_ref[i:i + 1, :] for i in range(4))
    r = _bdot(mix(0), wr_ref[...])
    k = _bdot(mix(1), wk_ref[...])
    v_ref[0] = _bdot(mix(2), wv_ref[...])
    g_ref[0] = _bdot(mix(3), wg_ref[...])
    wl = w0 + _bdot(jnp.tanh(_bdot(mix(4), w1_ref[...])), w2_ref[...])
    w_log = -(jnp.maximum(-wl, 0.0) + jnp.log(1.0 + jnp.exp(-jnp.abs(wl)))) - 0.5
    dec_ref[0] = jnp.exp(-jnp.exp(w_log))
    a = _sigmoid(a0 + _bdot(_bdot(mix(5), a1_ref[...]), a2_ref[...]))
    kk = k * k_k
    sq = kk * kk
    tot = sq[:, 0:LANES]
    for c in range(1, K_HI):
        tot = tot + sq[:, c * LANES:(c + 1) * LANES]
    inv = lax.rsqrt(jnp.maximum(_lane_group_sum(tot), L2_EPS_SQ))
    kkn = kk * jnp.concatenate([inv] * K_HI, axis=1)
    r_ref[0] = r
    k_ref[0] = k * (1.0 + (a - 1.0) * k_a)
    kk_ref[0] = kkn
    beta_ref[0] = kkn * a


def _rwkv_pre(x3, x_last, nw, mu8, wr, wk, wv, wg, w1, w2, a1, a2, vecs, tm, t_last):
    b, t, d = x3.shape
    tile = pl.BlockSpec((1, tm, d), lambda i, j: (i, j, 0))
    per_b = pl.BlockSpec((1, 1, d), lambda i, j: (i, 0, 0))
    f32o = jax.ShapeDtypeStruct((b, t, d), F32)
    return pl.pallas_call(
        functools.partial(_rwkv_pre_kernel, tm=tm, t_last=t_last),
        grid=(b, t // tm),
        in_specs=[tile, per_b, _const_spec((1, d)), _const_spec((8, d)),
                  _const_spec((d, d)), _const_spec((d, d)), _const_spec((d, d)),
                  _const_spec((d, d)), _const_spec((d, LANES)), _const_spec((LANES, d)),
                  _const_spec((d, LANES)), _const_spec((LANES, d)), _const_spec((8, d))],
        out_specs=[tile] * 7 + [per_b],
        out_shape=(f32o,) * 7 + (jax.ShapeDtypeStruct((b, 1, d), F32),),
        scratch_shapes=[pltpu.VMEM((1, d), F32)],
        compiler_params=_cparams("arbitrary", "arbitrary"),
        name="rwkv_pre",
    )(x3, x_last, nw, mu8, wr, wk, wv, wg, w1, w2, a1, a2, vecs)


def _wkv_kernel(r_ref, dec_ref, k_ref, kk_ref, beta_ref, v_ref, s0_ref, lnw_ref, lnb_ref, rk_ref,
                gs_ref, z_ref, sf_ref, *scs, nb, tb):
    s_scs, sa_scs, ya_scs, vc_scs, pend_scs = (scs[i * nb:(i + 1) * nb] for i in range(5))
    own_group = (lax.broadcasted_iota(jnp.int32, (SUBLANES, LANES), 1) // N_HEADS
                 == lax.broadcasted_iota(jnp.int32, (SUBLANES, LANES), 0))

    def group_sum(x):
        hi = x.astype(BF16)
        lo = (x - hi.astype(F32)).astype(BF16)
        return (jnp.dot(hi, gs_ref[...], preferred_element_type=F32)
                + jnp.dot(lo, gs_ref[...], preferred_element_type=F32))

    def expand(b, t):
        rows = [jnp.where(own_group, jnp.broadcast_to(v_ref[b, t, m:m + 1, :], (SUBLANES, LANES)), 0.0)
                for m in range(K_HI)]
        return group_sum(jnp.concatenate(rows, axis=0))

    def compact(x):
        rows = [jnp.sum(jnp.where(own_group, x[SUBLANES * m:SUBLANES * (m + 1)], 0.0), axis=0,
                        keepdims=True) for m in range(K_HI)]
        return jnp.concatenate(rows, axis=0)

    @pl.when(pl.program_id(1) == 0)
    def _():
        for b in range(nb):
            s_scs[b][...] = s0_ref[b]

    def rows_total(x):
        return _lane_group_sum(jnp.sum(x, axis=0, keepdims=True))

    def tree_sum(xs):
        while len(xs) > 1:
            xs = [xs[i] + xs[i + 1] for i in range(0, len(xs), 2)]
        return xs[0]

    def reduce_sa(b, t):
        return -group_sum(
            tree_sum([s_scs[b][c] * kk_ref[b, t, c:c + 1, :] for c in range(K_HI)]))

    def reduce_y(b, t):
        wr = dec_ref[b, t] * r_ref[b, t]
        ya_scs[b][...] = tree_sum([s_scs[b][c] * wr[c:c + 1, :] for c in range(K_HI)])

    def update(b, t, sa):
        w = dec_ref[b, t]
        km = k_ref[b, t]
        beta = beta_ref[b, t]
        vc = expand(b, t)
        for c in range(K_HI):
            s_scs[b][c] = (s_scs[b][c] * w[c:c + 1, :] + sa * beta[c:c + 1, :]
                           + vc * km[c:c + 1, :])
        sa_scs[b][...] = sa
        vc_scs[b][...] = vc

    def tail(b, t):
        r = r_ref[b, t]
        km = k_ref[b, t]
        vc = vc_scs[b][...]
        y = (group_sum(ya_scs[b][...]) + sa_scs[b][...] * rows_total(beta_ref[b, t] * r)
             + vc * rows_total(km * r))
        mean = jnp.sum(y, axis=0, keepdims=True) * (1.0 / HEAD_DIM)
        dy = y - mean
        var = jnp.sum(dy * dy, axis=0, keepdims=True) * (1.0 / HEAD_DIM)
        yn = dy * lax.rsqrt(var + GN_EPS)
        bonus = rows_total(r * km * rk_ref[...])
        z_ref[b, t] = compact(yn * lnw_ref[...] + lnb_ref[...] + vc * bonus)

    def step(t, carry):
        sa0 = reduce_sa(0, t)
        for b in range(1, nb):
            update(b, t - 1, pend_scs[b][...])
        for b in range(nb):
            tail(b, t - 1)
        reduce_y(0, t)
        for b in range(1, nb):
            pend_scs[b][...] = reduce_sa(b, t)
        update(0, t, sa0)
        for b in range(1, nb):
            reduce_y(b, t)
        return carry

    sa0 = reduce_sa(0, 0)
    for b in range(1, nb):
        pend_scs[b][...] = reduce_sa(b, 0)
    for b in range(nb):
        reduce_y(b, 0)
    update(0, 0, sa0)
    lax.fori_loop(1, tb, step, 0, unroll=3)
    for b in range(1, nb):
        update(b, tb - 1, pend_scs[b][...])
    for b in range(nb):
        tail(b, tb - 1)

    @pl.when(pl.program_id(1) == pl.num_programs(1) - 1)
    def _():
        for b in range(nb):
            sf_ref[b] = s_scs[b][...]


def _wkv_scan(r4, dec4, k4, kk4, beta4, v4, s0, lnw, lnb, rk, nb, tb):
    b, t = r4.shape[:2]
    lane = np.arange(LANES)
    same_head = jnp.asarray(lane[:, None] % N_HEADS == lane[None, :] % N_HEADS, dtype=BF16)
    rowf = pl.BlockSpec((nb, tb, K_HI, LANES), lambda i, j: (i, j, 0, 0))
    st = pl.BlockSpec((nb, K_HI, HEAD_DIM, LANES), lambda i, j: (i, 0, 0, 0))
    return pl.pallas_call(
        functools.partial(_wkv_kernel, nb=nb, tb=tb),
        grid=(b // nb, t // tb),
        in_specs=[rowf] * 6 + [st, _const_spec((HEAD_DIM, LANES)), _const_spec((HEAD_DIM, LANES)),
                               _const_spec((K_HI, LANES)), _const_spec((LANES, LANES))],
        out_specs=[rowf, st],
        out_shape=(jax.ShapeDtypeStruct((b, t, K_HI, LANES), F32),
                   jax.ShapeDtypeStruct((b, K_HI, HEAD_DIM, LANES), F32)),
        scratch_shapes=([pltpu.VMEM((K_HI, HEAD_DIM, LANES), F32) for _ in range(nb)]
                        + [pltpu.VMEM((HEAD_DIM, LANES), F32) for _ in range(4 * nb)]),
        compiler_params=_cparams("arbitrary", "arbitrary"),
        name="wkv_scan",
    )(r4, dec4, k4, kk4, beta4, v4, s0, lnw, lnb, rk, same_head)


def _pool_kernel(x_ref, hist_ref, nw_ref, win_ref, wgrp_ref, scale_ref, wout_ref,
                 y_ref, tail_ref, ext_ref, *, tm, t_valid, pos0):
    j = pl.program_id(1)

    @pl.when(j == 0)
    def _():
        ext_ref[0:POOL_HIST, :] = hist_ref[0]

    x = x_ref[0]
    xb = _rmsnorm_rows(x, nw_ref[...]).astype(BF16)
    u = jnp.dot(xb, win_ref[:, 0:D_MODEL], preferred_element_type=F32)
    g = jnp.dot(xb, win_ref[:, D_MODEL:2 * D_MODEL], preferred_element_type=F32)
    ext_ref[POOL_HIST:POOL_HIST + tm, :] = u
    pos = pos0 + j * tm + lax.broadcasted_iota(jnp.int32, (tm, POOL_GROUP), 0)
    mixed = []
    for gi, w in enumerate(POOL_WINDOWS):
        lo, hi = gi * POOL_GROUP, (gi + 1) * POOL_GROUP
        win = u[:, lo:hi]
        for dlt in range(1, w):
            win = win + ext_ref[POOL_HIST - dlt:POOL_HIST - dlt + tm, lo:hi]
        cnt = jnp.minimum(pos + 1, w).astype(F32)
        diff = win / cnt - u[:, lo:hi]
        mixed.append(_bdot(diff, wgrp_ref[gi]))
    mix = jnp.concatenate(mixed, axis=1) * scale_ref[...]
    z = mix * (g * _sigmoid(g))
    y_ref[0] = x + _bdot(z, wout_ref[...])
    tail = ext_ref[t_valid:t_valid + POOL_HIST, :]
    tail_ref[0] = tail
    ext_ref[0:POOL_HIST, :] = tail


def _pool_mixer(x3, hist, nw, w_in, w_grp, scale, w_out, tm, t_valid, pos0):
    b, t, d = x3.shape
    tile = pl.BlockSpec((1, tm, d), lambda i, j: (i, j, 0))
    per_b = pl.BlockSpec((1, POOL_HIST, d), lambda i, j: (i, 0, 0))
    return pl.pallas_call(
        functools.partial(_pool_kernel, tm=tm, t_valid=t_valid, pos0=pos0),
        grid=(b, t // tm),
        in_specs=[tile, per_b, _const_spec((1, d)), _const_spec((d, 2 * d)),
                  _const_spec((len(POOL_WINDOWS), POOL_GROUP, POOL_GROUP)),
                  _const_spec((1, d)), _const_spec((d, d))],
        out_specs=[tile, per_b],
        out_shape=(jax.ShapeDtypeStruct((b, t, d), F32),
                   jax.ShapeDtypeStruct((b, POOL_HIST, d), F32)),
        scratch_shapes=[pltpu.VMEM((POOL_HIST + tm, d), F32)],
        compiler_params=_cparams("arbitrary", "arbitrary"),
        name="pool_mixer",
    )(x3, hist, nw, w_in, w_grp, scale, w_out)


def _key_perm():
    n = np.arange(D_MODEL)
    kc, kl, h = n // LANES, (n % LANES) // N_HEADS, n % N_HEADS
    return h * HEAD_DIM + kc * SUBLANES + kl


def _pad_rows(x3, rows):
    b, t, d = x3.shape
    return jnp.concatenate([x3, jnp.zeros((b, rows - t, d), x3.dtype)], axis=1)


def _state_to_tiles(s):
    b = s.shape[0]
    s = s.reshape(b, N_HEADS, HEAD_DIM, K_HI, SUBLANES)
    return s.transpose(0, 3, 2, 4, 1).reshape(b, K_HI, HEAD_DIM, LANES)


def _tiles_to_state(s):
    b = s.shape[0]
    s = s.reshape(b, K_HI, HEAD_DIM, SUBLANES, N_HEADS)
    return s.transpose(0, 4, 2, 1, 3).reshape(b, N_HEADS, HEAD_DIM, HEAD_DIM)


def _head_col_tile(vec):
    return jnp.tile(vec.reshape(N_HEADS, HEAD_DIM).T, (1, SUBLANES))


def _fox_layer(j, n_fox, stacked, y_p, y_s, nw, cache_kt, cache_vt, cache_lft, page_table,
               fox_w_in, fox_b_f, fox_q_gain, fox_k_gain, fox_w_out, tri):
    d, h = D_MODEL, N_HEADS
    bp, tp, _ = y_p.shape
    bs, ts, _ = y_s.shape
    w_in = fox_w_in[j]
    w_main = w_in[:, :4 * d].astype(BF16)
    w_f = jnp.pad(w_in[:, 4 * d:], ((0, 0), (0, LANES - h))).astype(BF16)
    b_f = jnp.pad(fox_b_f[j], (0, LANES - h)).reshape(1, LANES)
    qg = jnp.tile(fox_q_gain[j], h).reshape(1, d)
    kg = jnp.tile(fox_k_gain[j], h).reshape(1, d)
    gmat = jnp.asarray(np.kron(np.eye(h), np.ones((HEAD_DIM, HEAD_DIM))), dtype=BF16)
    w_out = fox_w_out[j].astype(BF16)

    xp = y_p.reshape(bp * tp, d)
    qb, kb, vb, g, kt_st, vt_st, lft_st = _fox_proj_prompt(
        j, n_fox, y_p, nw, w_main, w_f, b_f, qg, kg, gmat, stacked, tm=min(tp, 256))
    cum = _cumsum_time(lft_st[j], tri)
    sh = (bp, tp, d)
    o = _fox_attention(qb.reshape(sh), kb.reshape(sh), vb.reshape(sh), cum, tq=min(tp, 512))
    y_p_new = _gated_out(o.reshape(bp * tp, d), g, xp, w_out, tm=512).reshape(bp, tp, d)
    new_p = (kt_st, vt_st, lft_st)

    xs = y_s.reshape(bs * ts, d)
    qb, k, kb, v, vb, g, lf = _fox_proj(xs, nw, w_main, w_f, b_f, qg, kg, gmat, tm=bs * ts)
    lf_s = lf[:, :h].reshape(bs, ts, h)
    head_of_lane = np.arange(d) // HEAD_DIM
    head_mask = jnp.asarray(head_of_lane[None, :] == np.arange(h)[:, None])
    q_bd = jnp.where(head_mask[None, None], qb.reshape(bs, ts, 1, d), jnp.zeros((), BF16))
    q_bd = q_bd.reshape(bs, ts * h, d)
    as_page = lambda z: jnp.pad(z.reshape(bs, ts, d).transpose(0, 2, 1),
                                ((0, 0), (0, 0), (0, PAGE_SIZE - ts)))
    lfn = jnp.pad(lf_s.transpose(0, 2, 1), ((0, 0), (0, 0), (0, PAGE_SIZE - ts)))
    o = _fox_decode(j, page_table, q_bd, cache_kt, cache_vt, cache_lft, as_page(kb), as_page(vb),
                    lfn, tri)
    y_s_new = _gated_out(o.reshape(bs * ts, d), g, xs, w_out, tm=bs * ts).reshape(bs, ts, d)
    new_s = (k.reshape(bs, ts, h, HEAD_DIM), v.reshape(bs, ts, h, HEAD_DIM), lf_s)
    return y_p_new, y_s_new, new_p, new_s


def _rwkv_group(y3, x_last, s0_tiles, nw, prm, tm, t_valid, nb, tb):
    b, t, d = y3.shape
    (mu8, wr, wk, wv, wg, w1, w2, a1, a2, vecs, rk, lnw, lnb, w_out) = prm
    r, dec, k, kk, beta, v, g, last = _rwkv_pre(y3, x_last, nw, mu8, wr, wk, wv, wg, w1, w2, a1, a2,
                                                vecs, tm=tm, t_last=(t_valid - 1) % tm)
    m = b * t
    rowf = lambda z: z[:, :t_valid].reshape(b, t_valid, K_HI, LANES)
    z, s_fin = _wkv_scan(rowf(r), rowf(dec), rowf(k), rowf(kk), rowf(beta), rowf(v), s0_tiles,
                         lnw, lnb, rk, nb=nb, tb=tb)
    z = z.reshape(b, t_valid, d)
    if t_valid != t:
        z = _pad_rows(z, t)
    tmo = min(m, 512)
    y_new = _gated_out(z.reshape(m, d), g.reshape(m, d), y3.reshape(m, d), w_out, tm=tmo)
    return y_new.reshape(b, t, d), s_fin, last[:, 0]


def _rwkv_params(j, rwkv_mu, rwkv_w_rkvg, rwkv_w0, rwkv_w1, rwkv_w2, rwkv_a0, rwkv_a1, rwkv_a2,
                 rwkv_k_k, rwkv_k_a, rwkv_r_k, rwkv_ln_w, rwkv_ln_b, rwkv_w_out):
    d = D_MODEL
    perm = _key_perm()
    lora = rwkv_w1.shape[-1]
    mu8 = jnp.pad(rwkv_mu[j], ((0, 2), (0, 0)))
    wr = rwkv_w_rkvg[j, 0][:, perm].astype(BF16)
    wk = rwkv_w_rkvg[j, 1][:, perm].astype(BF16)
    wv = rwkv_w_rkvg[j, 2][:, perm].astype(BF16)
    wg = rwkv_w_rkvg[j, 3][:, perm].astype(BF16)
    w1 = jnp.pad(rwkv_w1[j], ((0, 0), (0, LANES - lora))).astype(BF16)
    w2 = jnp.pad(rwkv_w2[j][:, perm], ((0, LANES - lora), (0, 0))).astype(BF16)
    a1 = jnp.pad(rwkv_a1[j], ((0, 0), (0, LANES - lora))).astype(BF16)
    a2 = jnp.pad(rwkv_a2[j][:, perm], ((0, LANES - lora), (0, 0))).astype(BF16)
    vecs = jnp.stack([rwkv_w0[j][perm], rwkv_a0[j][perm], rwkv_k_k[j][perm], rwkv_k_a[j][perm]])
    vecs = jnp.pad(vecs, ((0, 4), (0, 0)))
    rk = rwkv_r_k[j].reshape(d)[perm].reshape(K_HI, LANES)
    lnw = _head_col_tile(rwkv_ln_w[j])
    lnb = _head_col_tile(rwkv_ln_b[j])
    w_out = rwkv_w_out[j][perm, :].astype(BF16)
    return (mu8, wr, wk, wv, wg, w1, w2, a1, a2, vecs, rk, lnw, lnb, w_out)


def kernel(x_prompt, x_sample, cache_k, cache_v, cache_logf, page_table, state_wkv, state_shift,
           state_pool, norm_w, fox_w_in, fox_b_f, fox_q_gain, fox_k_gain, fox_w_out,
           rwkv_mu, rwkv_w_rkvg, rwkv_w0, rwkv_w1, rwkv_w2, rwkv_a0, rwkv_a1, rwkv_a2,
           rwkv_k_k, rwkv_k_a, rwkv_r_k, rwkv_ln_w, rwkv_ln_b, rwkv_w_out,
           pool_w_in, pool_w_grp, pool_scale, pool_w_out):
    d = D_MODEL
    depth = norm_w.shape[0]
    bp, tp, _ = x_prompt.shape
    bs, ts, _ = x_sample.shape
    past = page_table.shape[1] * PAGE_SIZE
    n_fox, n_pool_pages = cache_k.shape[:2]
    cache_kt = cache_k.transpose(0, 1, 3, 4, 2)
    cache_vt = cache_v.transpose(0, 1, 3, 4, 2)
    cache_lft = cache_logf.transpose(0, 1, 3, 2)
    tri = _tri_mats()
    ts_pad = SUBLANES * pl.cdiv(ts, SUBLANES)

    y_p, y_s = x_prompt, x_sample
    ks, vs, ls = [], [], []
    wkv_p, sh_p, wkv_s, sh_s, pool_p, pool_s = [], [], [], [], [], []
    n_fox_layers = (depth + 2) // 3
    prompt_kv = None
    for i in range(depth):
        kind, j = i % 3, i // 3
        nw = norm_w[i].reshape(1, d)
        if kind == 0:
            y_p, y_s, prompt_kv, new_s = _fox_layer(
                j, n_fox_layers, prompt_kv, y_p, y_s, nw, cache_kt, cache_vt, cache_lft,
                page_table, fox_w_in, fox_b_f, fox_q_gain, fox_k_gain, fox_w_out, tri)
            ks.append(new_s[0]); vs.append(new_s[1]); ls.append(new_s[2])
        elif kind == 1:
            prm = _rwkv_params(j, rwkv_mu, rwkv_w_rkvg, rwkv_w0, rwkv_w1, rwkv_w2, rwkv_a0,
                               rwkv_a1, rwkv_a2, rwkv_k_k, rwkv_k_a, rwkv_r_k, rwkv_ln_w,
                               rwkv_ln_b, rwkv_w_out)
            zero_state = jnp.zeros((bp, K_HI, HEAD_DIM, LANES), F32)
            y_p, s_fin, last = _rwkv_group(y_p, jnp.zeros((bp, 1, d), F32), zero_state, nw, prm,
                                           tm=min(tp, 256), t_valid=tp, nb=bp, tb=min(tp, 128))
            wkv_p.append(_tiles_to_state(s_fin)); sh_p.append(last)
            y_s_pad, s_fin, last = _rwkv_group(
                _pad_rows(y_s, ts_pad), state_shift[j].reshape(bs, 1, d),
                _state_to_tiles(state_wkv[j]), nw, prm, tm=ts_pad, t_valid=ts, nb=2, tb=ts)
            y_s = y_s_pad[:, :ts]
            wkv_s.append(_tiles_to_state(s_fin)); sh_s.append(last)
        else:
            w_in = pool_w_in[j].astype(BF16)
            w_grp = pool_w_grp[j].astype(BF16)
            scale = pool_scale[j].reshape(1, d)
            w_out = pool_w_out[j].astype(BF16)
            y_p, tail = _pool_mixer(y_p, jnp.zeros((bp, POOL_HIST, d), F32), nw, w_in, w_grp,
                                    scale, w_out, tm=min(tp, 256), t_valid=min(tp, 256), pos0=0)
            pool_p.append(tail[:, 1:])
            hist = jnp.pad(state_pool[j], ((0, 0), (1, 0), (0, 0)))
            y_s_pad, tail = _pool_mixer(_pad_rows(y_s, ts_pad), hist, nw, w_in, w_grp, scale,
                                        w_out, tm=ts_pad, t_valid=ts, pos0=past)
            y_s = y_s_pad[:, :ts]
            pool_s.append(tail[:, 1:])
    st = lambda xs: jnp.stack(xs, 0)
    kt_st, vt_st, lft_st = prompt_kv
    to_thd = lambda z: z.transpose(0, 1, 4, 2, 3)
    return (y_p, y_s, to_thd(kt_st), to_thd(vt_st), lft_st.transpose(0, 1, 3, 2),
            st(ks), st(vs), st(ls),
            st(wkv_p), st(sh_p), st(wkv_s), st(sh_s), st(pool_p), st(pool_s))
```

```python
import functools

import jax
import jax.numpy as jnp
import numpy as np
from jax import lax
from jax.experimental import pallas as pl
from jax.experimental.pallas import tpu as pltpu

F32 = jnp.float32
BF16 = jnp.bfloat16

D_MODEL = 1024
HEAD_DIM = 64
N_HEADS = D_MODEL // HEAD_DIM
PAGE_SIZE = 128
POOL_WINDOWS = (2, 4, 8, 16)
POOL_GROUP = D_MODEL // len(POOL_WINDOWS)
POOL_HIST = 16
RMS_EPS = 1e-6
GN_EPS = 64e-5
L2_EPS_SQ = 1e-24
ATTN_SCALE = HEAD_DIM ** -0.5
LOG2E = 1.4426950408889634

LANES = 128
SUBLANES = 8
K_HI = HEAD_DIM // SUBLANES
VMEM_LIMIT = 56 * 1024 * 1024
DECODE_PAGES_PER_STEP = 16
ATTN_HEADS_PER_STEP = 4
ATTN_HEAD_SETS = 4


def _cparams(*sem):
    return pltpu.CompilerParams(dimension_semantics=sem, vmem_limit_bytes=VMEM_LIMIT)


def _const_spec(shape):
    nd = len(shape)
    return pl.BlockSpec(shape, lambda *_: (0,) * nd, pipeline_mode=pl.Buffered(1))


def _rmsnorm_rows(x, w):
    ms = jnp.mean(x * x, axis=-1, keepdims=True)
    return x * lax.rsqrt(ms + RMS_EPS) * w


def _sigmoid(x):
    return 1.0 / (1.0 + jnp.exp(-x))


def _bdot(a, b):
    return jnp.dot(a.astype(BF16), b, preferred_element_type=F32)


def _dot3(x, m):
    hi = x.astype(BF16)
    r1 = x - hi.astype(F32)
    mid = r1.astype(BF16)
    lo = (r1 - mid.astype(F32)).astype(BF16)
    return (jnp.dot(hi, m, preferred_element_type=F32)
            + jnp.dot(mid, m, preferred_element_type=F32)
            + jnp.dot(lo, m, preferred_element_type=F32))


def _lane_group_sum(x):
    x = x + pltpu.roll(x, 16, axis=x.ndim - 1)
    x = x + pltpu.roll(x, 32, axis=x.ndim - 1)
    return x + pltpu.roll(x, 64, axis=x.ndim - 1)


def _fox_proj_kernel(x_ref, nw_ref, w_ref, wf_ref, bf_ref, qg_ref, kg_ref, gm_ref, *refs,
                     time_minor, own_slot=None):
    if time_minor:
        qb_ref, kb_ref, vb_ref, g_ref, kt_ref, vt_ref, lft_ref = refs[-7:]
    else:
        qb_ref, k_ref, kb_ref, v_ref, vb_ref, g_ref, lf_ref = refs
    xb = _rmsnorm_rows(x_ref[...], nw_ref[...]).astype(BF16)

    def headnorm(z, gain):
        ms = _bdot(z * z, gm_ref[...]) * (1.0 / HEAD_DIM)
        return z * lax.rsqrt(ms + RMS_EPS) * gain

    d = D_MODEL
    q = headnorm(jnp.dot(xb, w_ref[:, 0:d], preferred_element_type=F32), qg_ref[...])
    qb_ref[...] = (q * (ATTN_SCALE * LOG2E)).astype(BF16)
    k = headnorm(jnp.dot(xb, w_ref[:, d:2 * d], preferred_element_type=F32), kg_ref[...])
    kb_ref[...] = k.astype(BF16)
    v = jnp.dot(xb, w_ref[:, 2 * d:3 * d], preferred_element_type=F32)
    vb_ref[...] = v.astype(BF16)
    g_ref[...] = jnp.dot(xb, w_ref[:, 3 * d:4 * d], preferred_element_type=F32)
    fl = jnp.dot(xb, wf_ref[...], preferred_element_type=F32) + bf_ref[...]
    lf = -(jnp.maximum(-fl, 0.0) + jnp.log(1.0 + jnp.exp(-jnp.abs(fl))))
    if time_minor:
        tm = k.shape[0]
        if own_slot is not None:
            for ref in (kt_ref, vt_ref, lft_ref):
                ref[...] = jnp.zeros_like(ref)
        slot = 0 if own_slot is None else own_slot
        kt_ref[slot, 0] = k.T.reshape(N_HEADS, HEAD_DIM, tm)
        vt_ref[slot, 0] = v.T.reshape(N_HEADS, HEAD_DIM, tm)
        lft_ref[slot, 0] = lf.T[0:N_HEADS, :]
    else:
        k_ref[...] = k
        v_ref[...] = v
        lf_ref[...] = lf


def _fox_proj(x2d, nw, w_main, w_f, b_f, qg, kg, gmat, tm):
    m = x2d.shape[0]
    d = D_MODEL
    row = lambda n: pl.BlockSpec((tm, n), lambda i: (i, 0))
    outs = (jax.ShapeDtypeStruct((m, d), BF16), jax.ShapeDtypeStruct((m, d), F32),
            jax.ShapeDtypeStruct((m, d), BF16), jax.ShapeDtypeStruct((m, d), F32),
            jax.ShapeDtypeStruct((m, d), BF16), jax.ShapeDtypeStruct((m, d), F32),
            jax.ShapeDtypeStruct((m, LANES), F32))
    return pl.pallas_call(
        functools.partial(_fox_proj_kernel, time_minor=False),
        grid=(m // tm,),
        in_specs=[row(d), _const_spec((1, d)), _const_spec((d, 4 * d)), _const_spec((d, LANES)),
                  _const_spec((1, LANES)), _const_spec((1, d)), _const_spec((1, d)),
                  _const_spec((d, d))],
        out_specs=[row(d)] * 6 + [row(LANES)],
        out_shape=outs,
        compiler_params=_cparams("arbitrary"),
        name="fox_proj",
    )(x2d, nw, w_main, w_f, b_f, qg, kg, gmat)


def _fox_proj_prompt(layer, n_layers, x3, nw, w_main, w_f, b_f, qg, kg, gmat, stacked, tm):
    b, t, d = x3.shape
    nt = t // tm
    m = b * t
    row = pl.BlockSpec((tm, d), lambda i: (i, 0))
    own_all = stacked is None
    nl, l0 = (n_layers, 0) if own_all else (1, layer)
    kv_spec = pl.BlockSpec((nl, 1, N_HEADS, HEAD_DIM, tm), lambda i: (l0, i // nt, 0, 0, i % nt))
    lf_spec = pl.BlockSpec((nl, 1, N_HEADS, tm), lambda i: (l0, i // nt, 0, i % nt))
    kv_shape = jax.ShapeDtypeStruct((n_layers, b, N_HEADS, HEAD_DIM, t), F32)
    outs = (jax.ShapeDtypeStruct((m, d), BF16), jax.ShapeDtypeStruct((m, d), BF16),
            jax.ShapeDtypeStruct((m, d), BF16), jax.ShapeDtypeStruct((m, d), F32),
            kv_shape, kv_shape, jax.ShapeDtypeStruct((n_layers, b, N_HEADS, t), F32))
    in_specs = [row, _const_spec((1, d)), _const_spec((d, 4 * d)), _const_spec((d, LANES)),
                _const_spec((1, LANES)), _const_spec((1, d)), _const_spec((1, d)),
                _const_spec((d, d))]
    args = [x3.reshape(m, d), nw, w_main, w_f, b_f, qg, kg, gmat]
    aliases = {}
    if stacked is not None:
        in_specs += [pl.BlockSpec(memory_space=pl.ANY)] * 3
        aliases = {len(args) + i: 4 + i for i in range(3)}
        args += list(stacked)
    return pl.pallas_call(
        functools.partial(_fox_proj_kernel, time_minor=True, own_slot=layer if own_all else None),
        grid=(m // tm,),
        in_specs=in_specs,
        out_specs=[row] * 4 + [kv_spec, kv_spec, lf_spec],
        out_shape=outs,
        input_output_aliases=aliases,
        compiler_params=_cparams("arbitrary"),
        name="fox_proj_prompt",
    )(*args)


def _cumsum_kernel(x_ref, tri_ref, o_ref, carry_ref, *, nchunk):
    @pl.when(pl.program_id(1) == 0)
    def _():
        carry_ref[...] = jnp.zeros_like(carry_ref)

    carry = carry_ref[...]
    upper = tri_ref[0]
    ones = tri_ref[1]
    for c in range(nchunk):
        x = x_ref[0, :, c * LANES:(c + 1) * LANES]
        o_ref[0, :, c * LANES:(c + 1) * LANES] = (carry + _dot3(x, upper)) * LOG2E
        carry = carry + _dot3(x, ones)
    carry_ref[...] = carry


def _tri_mats():
    i = np.arange(LANES)
    upper = (i[:, None] <= i[None, :]).astype(np.float32)
    return jnp.asarray(np.stack([upper, np.ones_like(upper)]), dtype=BF16)


def _cumsum_time(x_bht, tri):
    b, h, t = x_bht.shape
    tc = min(t, 1024)
    return pl.pallas_call(
        functools.partial(_cumsum_kernel, nchunk=tc // LANES),
        grid=(b, t // tc),
        in_specs=[pl.BlockSpec((1, h, tc), lambda i, j: (i, 0, j)), _const_spec((2, LANES, LANES))],
        out_specs=pl.BlockSpec((1, h, tc), lambda i, j: (i, 0, j)),
        out_shape=jax.ShapeDtypeStruct((b, h, t), F32),
        scratch_shapes=[pltpu.VMEM((h, LANES), F32)],
        compiler_params=_cparams("arbitrary", "arbitrary"),
        name="logf_cumsum",
    )(x_bht, tri)


def _fox_attn_kernel(q_ref, k_ref, v_ref, c_ref, o_ref, s0_sc, s1_sc, m_sc, acc_sc, *, tq, nh):
    qi = pl.program_id(2)
    lane = lax.broadcasted_iota(jnp.int32, (tq, LANES), 1)
    halves = (lane < HEAD_DIM, lane >= HEAD_DIM)
    group = lambda ref_row, hh: ref_row[:, (hh // 2) * LANES:(hh // 2 + 1) * LANES]
    q_all = q_ref[0]
    qh = [jnp.where(halves[hh % 2], group(q_all, hh), jnp.zeros((), BF16)) for hh in range(nh)]

    def score(kj, dst, heads):
        k_all = k_ref[0, pl.ds(pl.multiple_of(kj * tq, tq), tq), :]
        for hh in heads:
            dst[hh] = lax.dot_general(qh[hh], group(k_all, hh), (((1,), (1,)), ((), ())),
                                      preferred_element_type=F32)

    def consume(kj, src, diagonal, heads):
        v_all = v_ref[0, pl.ds(pl.multiple_of(kj * tq, tq), tq), :]
        vh = {hh: jnp.where(halves[hh % 2], group(v_all, hh), jnp.ones((), BF16)) for hh in heads}
        cj = c_ref[0, 0, kj]
        for hh in heads:
            s = src[hh] - cj[hh:hh + 1, :]
            if diagonal:
                row = lax.broadcasted_iota(jnp.int32, (tq, tq), 0)
                col = lax.broadcasted_iota(jnp.int32, (tq, tq), 1)
                s = jnp.where(row >= col, s, -jnp.inf)
            part = s[:, 0:LANES]
            for c in range(1, tq // LANES):
                part = jnp.maximum(part, s[:, c * LANES:(c + 1) * LANES])
            m = m_sc[hh]
            m_new = jnp.maximum(m, jnp.max(part, axis=-1, keepdims=True))
            p = jnp.exp2(s - jnp.concatenate([m_new] * (tq // LANES), axis=1))
            acc_sc[hh] = jnp.exp2(m - m_new) * acc_sc[hh] + jnp.dot(
                p.astype(BF16), vh[hh], preferred_element_type=F32)
            m_sc[hh] = m_new

    m_sc[...] = jnp.full_like(m_sc, -jnp.inf)
    acc_sc[...] = jnp.zeros_like(acc_sc)
    sets = [tuple(range(i, nh, ATTN_HEAD_SETS)) for i in range(ATTN_HEAD_SETS)]
    score(0, s0_sc, sets[0])

    def block(j, cur, nxt, diagonal):
        for i, heads in enumerate(sets):
            if i + 1 < len(sets):
                score(j, cur, sets[i + 1])
            elif nxt is not None:
                score(j + 1, nxt, sets[0])
            consume(j, cur, diagonal, heads)

    def pair(pp, carry):
        block(2 * pp, s0_sc, s1_sc, False)
        block(2 * pp + 1, s1_sc, s0_sc, False)
        return carry

    lax.fori_loop(0, qi // 2, pair, 0)

    @pl.when(qi % 2 == 1)
    def _():
        block(qi - 1, s0_sc, s1_sc, False)
        block(qi, s1_sc, None, True)

    @pl.when(qi % 2 == 0)
    def _():
        block(qi, s0_sc, None, True)

    outs = []
    for g in range(nh // 2):
        a0, a1 = acc_sc[2 * g], acc_sc[2 * g + 1]
        outs.append(jnp.where(halves[0], a0 / pltpu.roll(a0, HEAD_DIM, axis=1),
                              a1 / pltpu.roll(a1, HEAD_DIM, axis=1)))
    o_ref[0] = jnp.concatenate(outs, axis=1)


def _fox_attention(qb, kb, vb, cum_bht, tq):
    b, t, d = qb.shape
    nh = ATTN_HEADS_PER_STEP
    hg = N_HEADS // nh
    w = nh * HEAD_DIM
    nt = t // tq
    cum5 = cum_bht.reshape(b, hg, nh, nt, tq).transpose(0, 1, 3, 2, 4)
    return pl.pallas_call(
        functools.partial(_fox_attn_kernel, tq=tq, nh=nh),
        grid=(b, hg, nt),
        in_specs=[pl.BlockSpec((1, tq, w), lambda i, h, q: (i, q, h)),
                  pl.BlockSpec((1, t, w), lambda i, h, q: (i, 0, h)),
                  pl.BlockSpec((1, t, w), lambda i, h, q: (i, 0, h)),
                  pl.BlockSpec((1, 1, nt, nh, tq), lambda i, h, q: (i, h, 0, 0, 0))],
        out_specs=pl.BlockSpec((1, tq, w), lambda i, h, q: (i, q, h)),
        out_shape=jax.ShapeDtypeStruct((b, t, d), F32),
        scratch_shapes=[pltpu.VMEM((nh, tq, tq), F32), pltpu.VMEM((nh, tq, tq), F32),
                        pltpu.VMEM((nh, tq, LANES), F32), pltpu.VMEM((nh, tq, LANES), F32)],
        compiler_params=_cparams("arbitrary", "arbitrary", "arbitrary"),
        name="fox_prompt_attention",
    )(qb, kb, vb, cum5)


def _fox_decode_kernel(pt_ref, q_ref, *refs, n_q, n_par):
    del pt_ref
    k_refs, v_refs, lf_refs = refs[:n_par], refs[n_par:2 * n_par], refs[2 * n_par:3 * n_par]
    kn_ref, vn_ref, lfn_ref, tri_ref, o_ref, m_sc, l_sc, acc_sc, carry_sc = refs[3 * n_par:]
    p = pl.program_id(1)
    rows = n_q * N_HEADS

    @pl.when(p == 0)
    def _():
        m_sc[...] = jnp.full_like(m_sc, -jnp.inf)
        l_sc[...] = jnp.zeros_like(l_sc)
        acc_sc[...] = jnp.zeros_like(acc_sc)
        carry_sc[...] = jnp.zeros_like(carry_sc)

    q = q_ref[0]

    def cumulate(lft):
        cum = carry_sc[...] + _dot3(lft, tri_ref[0])
        carry_sc[...] = carry_sc[...] + _dot3(lft, tri_ref[1])
        return cum * LOG2E

    def update(kts, vts, cums, ok):
        ss = []
        for kt, cum in zip(kts, cums):
            s = jnp.dot(q, kt, preferred_element_type=F32) - jnp.concatenate([cum] * n_q, axis=0)
            ss.append(s if ok is None else jnp.where(ok, s, -jnp.inf))
        m = m_sc[...]
        m_new = m
        for s in ss:
            m_new = jnp.maximum(m_new, jnp.max(s, axis=-1, keepdims=True))
        alpha = jnp.exp2(m - m_new)
        l = alpha * l_sc[...]
        acc = alpha * acc_sc[...]
        for s, vt in zip(ss, vts):
            pr = jnp.exp2(s - m_new)
            l = l + jnp.sum(pr, axis=-1, keepdims=True)
            acc = acc + lax.dot_general(pr.astype(BF16), vt, (((1,), (1,)), ((), ())),
                                        preferred_element_type=F32)
        l_sc[...] = l
        acc_sc[...] = acc
        m_sc[...] = m_new

    update([r[0, 0].reshape(D_MODEL, PAGE_SIZE).astype(BF16) for r in k_refs],
           [r[0, 0].reshape(D_MODEL, PAGE_SIZE).astype(BF16) for r in v_refs],
           [cumulate(r[0, 0]) for r in lf_refs], None)

    @pl.when(p == pl.num_programs(1) - 1)
    def _():
        r = lax.broadcasted_iota(jnp.int32, (rows, PAGE_SIZE), 0)
        j = lax.broadcasted_iota(jnp.int32, (rows, PAGE_SIZE), 1)
        update([kn_ref[0]], [vn_ref[0]], [cumulate(lfn_ref[0])], j <= r // N_HEADS)
        o = acc_sc[...] / l_sc[...]
        rr = lax.broadcasted_iota(jnp.int32, (rows, D_MODEL), 0)
        cc = lax.broadcasted_iota(jnp.int32, (rows, D_MODEL), 1)
        o = jnp.where(rr % N_HEADS == cc // HEAD_DIM, o, 0.0)
        o_ref[0] = jnp.concatenate(
            [jnp.sum(o[N_HEADS * t:N_HEADS * (t + 1)], axis=0, keepdims=True) for t in range(n_q)],
            axis=0)


def _fox_decode(layer, page_table, q_bd, cache_kt, cache_vt, cache_lft, knt, vnt, lfn, tri):
    b, rows, d = q_bd.shape
    n_q = rows // N_HEADS
    n_pages = page_table.shape[1]
    n_par = DECODE_PAGES_PER_STEP if n_pages % DECODE_PAGES_PER_STEP == 0 else 1
    page5 = lambda u: (lambda i, p, pt: (layer, pt[i, p * n_par + u], 0, 0, 0))
    page4 = lambda u: (lambda i, p, pt: (layer, pt[i, p * n_par + u], 0, 0))
    per_b = lambda i, p, pt: (i, 0, 0)
    kv_specs = [pl.BlockSpec((1, 1, N_HEADS, HEAD_DIM, PAGE_SIZE), page5(u)) for u in range(n_par)]
    lf_specs = [pl.BlockSpec((1, 1, N_HEADS, PAGE_SIZE), page4(u)) for u in range(n_par)]
    grid_spec = pltpu.PrefetchScalarGridSpec(
        num_scalar_prefetch=1,
        grid=(b, n_pages // n_par),
        in_specs=[pl.BlockSpec((1, rows, d), per_b)] + kv_specs + kv_specs + lf_specs + [
            pl.BlockSpec((1, d, PAGE_SIZE), per_b),
            pl.BlockSpec((1, d, PAGE_SIZE), per_b),
            pl.BlockSpec((1, N_HEADS, PAGE_SIZE), per_b),
            pl.BlockSpec((2, LANES, LANES), lambda i, p, pt: (0, 0, 0))],
        out_specs=pl.BlockSpec((1, n_q, d), per_b),
        scratch_shapes=[pltpu.VMEM((rows, 1), F32), pltpu.VMEM((rows, 1), F32),
                        pltpu.VMEM((rows, d), F32), pltpu.VMEM((N_HEADS, LANES), F32)])
    return pl.pallas_call(
        functools.partial(_fox_decode_kernel, n_q=n_q, n_par=n_par),
        grid_spec=grid_spec,
        out_shape=jax.ShapeDtypeStruct((b, n_q, d), F32),
        compiler_params=_cparams("arbitrary", "arbitrary"),
        name="fox_sample_attention",
    )(page_table, q_bd, *([cache_kt] * n_par), *([cache_vt] * n_par), *([cache_lft] * n_par),
      knt, vnt, lfn, tri)


def _gated_out_kernel(o_ref, g_ref, res_ref, w_ref, y_ref):
    g = g_ref[...]
    z = o_ref[...] * (g * _sigmoid(g))
    y_ref[...] = res_ref[...] + _bdot(z, w_ref[...])


def _gated_out(o2d, g2d, res2d, w_out, tm):
    m, d = o2d.shape
    row = pl.BlockSpec((tm, d), lambda i: (i, 0))
    return pl.pallas_call(
        _gated_out_kernel,
        grid=(m // tm,),
        in_specs=[row, row, row, _const_spec((d, d))],
        out_specs=row,
        out_shape=jax.ShapeDtypeStruct((m, d), F32),
        compiler_params=_cparams("arbitrary"),
        name="gated_out_proj",
    )(o2d, g2d, res2d, w_out)


def _rwkv_pre_kernel(x_ref, xlast_ref, nw_ref, mu_ref, wr_ref, wk_ref, wv_ref, wg_ref, w1_ref,
                     w2_ref, a1_ref, a2_ref, vec_ref,
                     r_ref, dec_ref, k_ref, kk_ref, beta_ref, v_ref, g_ref, last_ref,
                     carry_ref, *, tm, t_last):
    @pl.when(pl.program_id(1) == 0)
    def _():
        carry_ref[...] = xlast_ref[0]

    xn = _rmsnorm_rows(x_ref[0], nw_ref[...])
    rowi = lax.broadcasted_iota(jnp.int32, (tm, D_MODEL), 0)
    prev = jnp.where(rowi == 0, carry_ref[...], pltpu.roll(xn, 1, axis=0))
    carry_ref[...] = xn[tm - 1:tm, :]
    last_ref[0] = xn[t_last:t_last + 1, :]
    dx = prev - xn
    mix = lambda j: xn + dx * mu_ref[j:j + 1, :]

    w0, a0, k_k, k_a = (vec_ref[i:i + 1, :] for i in range(4))
    r = _bdot(mix(0), wr_ref[...])
    k = _bdot(mix(1), wk_ref[...])
    v_ref[0] = _bdot(mix(2), wv_ref[...])
    g_ref[0] = _bdot(mix(3), wg_ref[...])
    wl = w0 + _bdot(jnp.tanh(_bdot(mix(4), w1_ref[...])), w2_ref[...])
    w_log = -(jnp.maximum(-wl, 0.0) + jnp.log(1.0 + jnp.exp(-jnp.abs(wl)))) - 0.5
    dec_ref[0] = jnp.exp(-jnp.exp(w_log))
    a = _sigmoid(a0 + _bdot(_bdot(mix(5), a1_ref[...]), a2_ref[...]))
    kk = k * k_k
    sq = kk * kk
    tot = sq[:, 0:LANES]
    for c in range(1, K_HI):
        tot = tot + sq[:, c * LANES:(c + 1) * LANES]
    inv = lax.rsqrt(jnp.maximum(_lane_group_sum(tot), L2_EPS_SQ))
    kkn = kk * jnp.concatenate([inv] * K_HI, axis=1)
    r_ref[0] = r
    k_ref[0] = k * (1.0 + (a - 1.0) * k_a)
    kk_ref[0] = kkn
    beta_ref[0] = kkn * a


def _rwkv_pre(x3, x_last, nw, mu8, wr, wk, wv, wg, w1, w2, a1, a2, vecs, tm, t_last):
    b, t, d = x3.shape
    tile = pl.BlockSpec((1, tm, d), lambda i, j: (i, j, 0))
    per_b = pl.BlockSpec((1, 1, d), lambda i, j: (i, 0, 0))
    f32o = jax.ShapeDtypeStruct((b, t, d), F32)
    return pl.pallas_call(
        functools.partial(_rwkv_pre_kernel, tm=tm, t_last=t_last),
        grid=(b, t // tm),
        in_specs=[tile, per_b, _const_spec((1, d)), _const_spec((8, d)),
                  _const_spec((d, d)), _const_spec((d, d)), _const_spec((d, d)),
                  _const_spec((d, d)), _const_spec((d, LANES)), _const_spec((LANES, d)),
                  _const_spec((d, LANES)), _const_spec((LANES, d)), _const_spec((8, d))],
        out_specs=[tile] * 7 + [per_b],
        out_shape=(f32o,) * 7 + (jax.ShapeDtypeStruct((b, 1, d), F32),),
        scratch_shapes=[pltpu.VMEM((1, d), F32)],
        compiler_params=_cparams("arbitrary", "arbitrary"),
        name="rwkv_pre",
    )(x3, x_last, nw, mu8, wr, wk, wv, wg, w1, w2, a1, a2, vecs)


def _wkv_kernel(r_ref, dec_ref, k_ref, kk_ref, beta_ref, v_ref, s0_ref, lnw_ref, lnb_ref, rk_ref,
                gs_ref, z_ref, sf_ref, *scs, nb, tb):
    s_scs, sa_scs, ya_scs, vc_scs, pend_scs = (scs[i * nb:(i + 1) * nb] for i in range(5))
    own_group = (lax.broadcasted_iota(jnp.int32, (SUBLANES, LANES), 1) // N_HEADS
                 == lax.broadcasted_iota(jnp.int32, (SUBLANES, LANES), 0))

    def group_sum(x):
        hi = x.astype(BF16)
        lo = (x - hi.astype(F32)).astype(BF16)
        return (jnp.dot(hi, gs_ref[...], preferred_element_type=F32)
                + jnp.dot(lo, gs_ref[...], preferred_element_type=F32))

    def expand(b, t):
        rows = [jnp.where(own_group, jnp.broadcast_to(v_ref[b, t, m:m + 1, :], (SUBLANES, LANES)), 0.0)
                for m in range(K_HI)]
        return group_sum(jnp.concatenate(rows, axis=0))

    def compact(x):
        rows = [jnp.sum(jnp.where(own_group, x[SUBLANES * m:SUBLANES * (m + 1)], 0.0), axis=0,
                        keepdims=True) for m in range(K_HI)]
        return jnp.concatenate(rows, axis=0)

    @pl.when(pl.program_id(1) == 0)
    def _():
        for b in range(nb):
            s_scs[b][...] = s0_ref[b]

    def rows_total(x):
        return _lane_group_sum(jnp.sum(x, axis=0, keepdims=True))

    def tree_sum(xs):
        while len(xs) > 1:
            xs = [xs[i] + xs[i + 1] for i in range(0, len(xs), 2)]
        return xs[0]

    def reduce_sa(b, t):
        return -group_sum(
            tree_sum([s_scs[b][c] * kk_ref[b, t, c:c + 1, :] for c in range(K_HI)]))

    def reduce_y(b, t):
        wr = dec_ref[b, t] * r_ref[b, t]
        ya_scs[b][...] = tree_sum([s_scs[b][c] * wr[c:c + 1, :] for c in range(K_HI)])

    def update(b, t, sa):
        w = dec_ref[b, t]
        km = k_ref[b, t]
        beta = beta_ref[b, t]
        vc = expand(b, t)
        for c in range(K_HI):
            s_scs[b][c] = (s_scs[b][c] * w[c:c + 1, :] + sa * beta[c:c + 1, :]
                           + vc * km[c:c + 1, :])
        sa_scs[b][...] = sa
        vc_scs[b][...] = vc

    def tail(b, t):
        r = r_ref[b, t]
        km = k_ref[b, t]
        vc = vc_scs[b][...]
        y = (group_sum(ya_scs[b][...]) + sa_scs[b][...] * rows_total(beta_ref[b, t] * r)
             + vc * rows_total(km * r))
        mean = jnp.sum(y, axis=0, keepdims=True) * (1.0 / HEAD_DIM)
        dy = y - mean
        var = jnp.sum(dy * dy, axis=0, keepdims=True) * (1.0 / HEAD_DIM)
        yn = dy * lax.rsqrt(var + GN_EPS)
        bonus = rows_total(r * km * rk_ref[...])
        z_ref[b, t] = compact(yn * lnw_ref[...] + lnb_ref[...] + vc * bonus)

    def step(t, carry):
        sa0 = reduce_sa(0, t)
        for b in range(1, nb):
            update(b, t - 1, pend_scs[b][...])
        for b in range(nb):
            tail(b, t - 1)
        reduce_y(0, t)
        for b in range(1, nb):
            pend_scs[b][...] = reduce_sa(b, t)
        update(0, t, sa0)
        for b in range(1, nb):
            reduce_y(b, t)
        return carry

    sa0 = reduce_sa(0, 0)
    for b in range(1, nb):
        pend_scs[b][...] = reduce_sa(b, 0)
    for b in range(nb):
        reduce_y(b, 0)
    update(0, 0, sa0)
    lax.fori_loop(1, tb, step, 0, unroll=3)
    for b in range(1, nb):
        update(b, tb - 1, pend_scs[b][...])
    for b in range(nb):
        tail(b, tb - 1)

    @pl.when(pl.program_id(1) == pl.num_programs(1) - 1)
    def _():
        for b in range(nb):
            sf_ref[b] = s_scs[b][...]


def _wkv_scan(r4, dec4, k4, kk4, beta4, v4, s0, lnw, lnb, rk, nb, tb):
    b, t = r4.shape[:2]
    lane = np.arange(LANES)
    same_head = jnp.asarray(lane[:, None] % N_HEADS == lane[None, :] % N_HEADS, dtype=BF16)
    rowf = pl.BlockSpec((nb, tb, K_HI, LANES), lambda i, j: (i, j, 0, 0))
    st = pl.BlockSpec((nb, K_HI, HEAD_DIM, LANES), lambda i, j: (i, 0, 0, 0))
    return pl.pallas_call(
        functools.partial(_wkv_kernel, nb=nb, tb=tb),
        grid=(b // nb, t // tb),
        in_specs=[rowf] * 6 + [st, _const_spec((HEAD_DIM, LANES)), _const_spec((HEAD_DIM, LANES)),
                               _const_spec((K_HI, LANES)), _const_spec((LANES, LANES))],
        out_specs=[rowf, st],
        out_shape=(jax.ShapeDtypeStruct((b, t, K_HI, LANES), F32),
                   jax.ShapeDtypeStruct((b, K_HI, HEAD_DIM, LANES), F32)),
        scratch_shapes=([pltpu.VMEM((K_HI, HEAD_DIM, LANES), F32) for _ in range(nb)]
                        + [pltpu.VMEM((HEAD_DIM, LANES), F32) for _ in range(4 * nb)]),
        compiler_params=_cparams("arbitrary", "arbitrary"),
        name="wkv_scan",
    )(r4, dec4, k4, kk4, beta4, v4, s0, lnw, lnb, rk, same_head)


def _pool_kernel(x_ref, hist_ref, nw_ref, win_ref, wgrp_ref, scale_ref, wout_ref,
                 y_ref, tail_ref, ext_ref, *, tm, t_valid, pos0):
    j = pl.program_id(1)

    @pl.when(j == 0)
    def _():
        ext_ref[0:POOL_HIST, :] = hist_ref[0]

    x = x_ref[0]
    xb = _rmsnorm_rows(x, nw_ref[...]).astype(BF16)
    u = jnp.dot(xb, win_ref[:, 0:D_MODEL], preferred_element_type=F32)
    g = jnp.dot(xb, win_ref[:, D_MODEL:2 * D_MODEL], preferred_element_type=F32)
    ext_ref[POOL_HIST:POOL_HIST + tm, :] = u
    pos = pos0 + j * tm + lax.broadcasted_iota(jnp.int32, (tm, POOL_GROUP), 0)
    mixed = []
    for gi, w in enumerate(POOL_WINDOWS):
        lo, hi = gi * POOL_GROUP, (gi + 1) * POOL_GROUP
        win = u[:, lo:hi]
        for dlt in range(1, w):
            win = win + ext_ref[POOL_HIST - dlt:POOL_HIST - dlt + tm, lo:hi]
        cnt = jnp.minimum(pos + 1, w).astype(F32)
        diff = win / cnt - u[:, lo:hi]
        mixed.append(_bdot(diff, wgrp_ref[gi]))
    mix = jnp.concatenate(mixed, axis=1) * scale_ref[...]
    z = mix * (g * _sigmoid(g))
    y_ref[0] = x + _bdot(z, wout_ref[...])
    tail = ext_ref[t_valid:t_valid + POOL_HIST, :]
    tail_ref[0] = tail
    ext_ref[0:POOL_HIST, :] = tail


def _pool_mixer(x3, hist, nw, w_in, w_grp, scale, w_out, tm, t_valid, pos0):
    b, t, d = x3.shape
    tile = pl.BlockSpec((1, tm, d), lambda i, j: (i, j, 0))
    per_b = pl.BlockSpec((1, POOL_HIST, d), lambda i, j: (i, 0, 0))
    return pl.pallas_call(
        functools.partial(_pool_kernel, tm=tm, t_valid=t_valid, pos0=pos0),
        grid=(b, t // tm),
        in_specs=[tile, per_b, _const_spec((1, d)), _const_spec((d, 2 * d)),
                  _const_spec((len(POOL_WINDOWS), POOL_GROUP, POOL_GROUP)),
                  _const_spec((1, d)), _const_spec((d, d))],
        out_specs=[tile, per_b],
        out_shape=(jax.ShapeDtypeStruct((b, t, d), F32),
                   jax.ShapeDtypeStruct((b, POOL_HIST, d), F32)),
        scratch_shapes=[pltpu.VMEM((POOL_HIST + tm, d), F32)],
        compiler_params=_cparams("arbitrary", "arbitrary"),
        name="pool_mixer",
    )(x3, hist, nw, w_in, w_grp, scale, w_out)


def _key_perm():
    n = np.arange(D_MODEL)
    kc, kl, h = n // LANES, (n % LANES) // N_HEADS, n % N_HEADS
    return h * HEAD_DIM + kc * SUBLANES + kl


def _pad_rows(x3, rows):
    b, t, d = x3.shape
    return jnp.concatenate([x3, jnp.zeros((b, rows - t, d), x3.dtype)], axis=1)


def _state_to_tiles(s):
    b = s.shape[0]
    s = s.reshape(b, N_HEADS, HEAD_DIM, K_HI, SUBLANES)
    return s.transpose(0, 3, 2, 4, 1).reshape(b, K_HI, HEAD_DIM, LANES)


def _tiles_to_state(s):
    b = s.shape[0]
    s = s.reshape(b, K_HI, HEAD_DIM, SUBLANES, N_HEADS)
    return s.transpose(0, 4, 2, 1, 3).reshape(b, N_HEADS, HEAD_DIM, HEAD_DIM)


def _head_col_tile(vec):
    return jnp.tile(vec.reshape(N_HEADS, HEAD_DIM).T, (1, SUBLANES))


def _fox_layer(j, n_fox, stacked, y_p, y_s, nw, cache_kt, cache_vt, cache_lft, page_table,
               fox_w_in, fox_b_f, fox_q_gain, fox_k_gain, fox_w_out, tri):
    d, h = D_MODEL, N_HEADS
    bp, tp, _ = y_p.shape
    bs, ts, _ = y_s.shape
    w_in = fox_w_in[j]
    w_main = w_in[:, :4 * d].astype(BF16)
    w_f = jnp.pad(w_in[:, 4 * d:], ((0, 0), (0, LANES - h))).astype(BF16)
    b_f = jnp.pad(fox_b_f[j], (0, LANES - h)).reshape(1, LANES)
    qg = jnp.tile(fox_q_gain[j], h).reshape(1, d)
    kg = jnp.tile(fox_k_gain[j], h).reshape(1, d)
    gmat = jnp.asarray(np.kron(np.eye(h), np.ones((HEAD_DIM, HEAD_DIM))), dtype=BF16)
    w_out = fox_w_out[j].astype(BF16)

    xp = y_p.reshape(bp * tp, d)
    qb, kb, vb, g, kt_st, vt_st, lft_st = _fox_proj_prompt(
        j, n_fox, y_p, nw, w_main, w_f, b_f, qg, kg, gmat, stacked, tm=min(tp, 256))
    cum = _cumsum_time(lft_st[j], tri)
    sh = (bp, tp, d)
    o = _fox_attention(qb.reshape(sh), kb.reshape(sh), vb.reshape(sh), cum, tq=min(tp, 512))
    y_p_new = _gated_out(o.reshape(bp * tp, d), g, xp, w_out, tm=512).reshape(bp, tp, d)
    new_p = (kt_st, vt_st, lft_st)

    xs = y_s.reshape(bs * ts, d)
    qb, k, kb, v, vb, g, lf = _fox_proj(xs, nw, w_main, w_f, b_f, qg, kg, gmat, tm=bs * ts)
    lf_s = lf[:, :h].reshape(bs, ts, h)
    head_of_lane = np.arange(d) // HEAD_DIM
    head_mask = jnp.asarray(head_of_lane[None, :] == np.arange(h)[:, None])
    q_bd = jnp.where(head_mask[None, None], qb.reshape(bs, ts, 1, d), jnp.zeros((), BF16))
    q_bd = q_bd.reshape(bs, ts * h, d)
    as_page = lambda z: jnp.pad(z.reshape(bs, ts, d).transpose(0, 2, 1),
                                ((0, 0), (0, 0), (0, PAGE_SIZE - ts)))
    lfn = jnp.pad(lf_s.transpose(0, 2, 1), ((0, 0), (0, 0), (0, PAGE_SIZE - ts)))
    o = _fox_decode(j, page_table, q_bd, cache_kt, cache_vt, cache_lft, as_page(kb), as_page(vb),
                    lfn, tri)
    y_s_new = _gated_out(o.reshape(bs * ts, d), g, xs, w_out, tm=bs * ts).reshape(bs, ts, d)
    new_s = (k.reshape(bs, ts, h, HEAD_DIM), v.reshape(bs, ts, h, HEAD_DIM), lf_s)
    return y_p_new, y_s_new, new_p, new_s


def _rwkv_group(y3, x_last, s0_tiles, nw, prm, tm, t_valid, nb, tb):
    b, t, d = y3.shape
    (mu8, wr, wk, wv, wg, w1, w2, a1, a2, vecs, rk, lnw, lnb, w_out) = prm
    r, dec, k, kk, beta, v, g, last = _rwkv_pre(y3, x_last, nw, mu8, wr, wk, wv, wg, w1, w2, a1, a2,
                                                vecs, tm=tm, t_last=(t_valid - 1) % tm)
    m = b * t
    rowf = lambda z: z[:, :t_valid].reshape(b, t_valid, K_HI, LANES)
    z, s_fin = _wkv_scan(rowf(r), rowf(dec), rowf(k), rowf(kk), rowf(beta), rowf(v), s0_tiles,
                         lnw, lnb, rk, nb=nb, tb=tb)
    z = z.reshape(b, t_valid, d)
    if t_valid != t:
        z = _pad_rows(z, t)
    tmo = min(m, 512)
    y_new = _gated_out(z.reshape(m, d), g.reshape(m, d), y3.reshape(m, d), w_out, tm=tmo)
    return y_new.reshape(b, t, d), s_fin, last[:, 0]


def _rwkv_params(j, rwkv_mu, rwkv_w_rkvg, rwkv_w0, rwkv_w1, rwkv_w2, rwkv_a0, rwkv_a1, rwkv_a2,
                 rwkv_k_k, rwkv_k_a, rwkv_r_k, rwkv_ln_w, rwkv_ln_b, rwkv_w_out):
    d = D_MODEL
    perm = _key_perm()
    lora = rwkv_w1.shape[-1]
    mu8 = jnp.pad(rwkv_mu[j], ((0, 2), (0, 0)))
    wr = rwkv_w_rkvg[j, 0][:, perm].astype(BF16)
    wk = rwkv_w_rkvg[j, 1][:, perm].astype(BF16)
    wv = rwkv_w_rkvg[j, 2][:, perm].astype(BF16)
    wg = rwkv_w_rkvg[j, 3][:, perm].astype(BF16)
    w1 = jnp.pad(rwkv_w1[j], ((0, 0), (0, LANES - lora))).astype(BF16)
    w2 = jnp.pad(rwkv_w2[j][:, perm], ((0, LANES - lora), (0, 0))).astype(BF16)
    a1 = jnp.pad(rwkv_a1[j], ((0, 0), (0, LANES - lora))).astype(BF16)
    a2 = jnp.pad(rwkv_a2[j][:, perm], ((0, LANES - lora), (0, 0))).astype(BF16)
    vecs = jnp.stack([rwkv_w0[j][perm], rwkv_a0[j][perm], rwkv_k_k[j][perm], rwkv_k_a[j][perm]])
    vecs = jnp.pad(vecs, ((0, 4), (0, 0)))
    rk = rwkv_r_k[j].reshape(d)[perm].reshape(K_HI, LANES)
    lnw = _head_col_tile(rwkv_ln_w[j])
    lnb = _head_col_tile(rwkv_ln_b[j])
    w_out = rwkv_w_out[j][perm, :].astype(BF16)
    return (mu8, wr, wk, wv, wg, w1, w2, a1, a2, vecs, rk, lnw, lnb, w_out)


def kernel(x_prompt, x_sample, cache_k, cache_v, cache_logf, page_table, state_wkv, state_shift,
           state_pool, norm_w, fox_w_in, fox_b_f, fox_q_gain, fox_k_gain, fox_w_out,
           rwkv_mu, rwkv_w_rkvg, rwkv_w0, rwkv_w1, rwkv_w2, rwkv_a0, rwkv_a1, rwkv_a2,
           rwkv_k_k, rwkv_k_a, rwkv_r_k, rwkv_ln_w, rwkv_ln_b, rwkv_w_out,
           pool_w_in, pool_w_grp, pool_scale, pool_w_out):
    d = D_MODEL
    depth = norm_w.shape[0]
    bp, tp, _ = x_prompt.shape
    bs, ts, _ = x_sample.shape
    past = page_table.shape[1] * PAGE_SIZE
    n_fox, n_pool_pages = cache_k.shape[:2]
    cache_kt = cache_k.transpose(0, 1, 3, 4, 2)
    cache_vt = cache_v.transpose(0, 1, 3, 4, 2)
    cache_lft = cache_logf.transpose(0, 1, 3, 2)
    tri = _tri_mats()
    ts_pad = SUBLANES * pl.cdiv(ts, SUBLANES)

    y_p, y_s = x_prompt, x_sample
    ks, vs, ls = [], [], []
    wkv_p, sh_p, wkv_s, sh_s, pool_p, pool_s = [], [], [], [], [], []
    n_fox_layers = (depth + 2) // 3
    prompt_kv = None
    for i in range(depth):
        kind, j = i % 3, i // 3
        nw = norm_w[i].reshape(1, d)
        if kind == 0:
            y_p, y_s, prompt_kv, new_s = _fox_layer(
                j, n_fox_layers, prompt_kv, y_p, y_s, nw, cache_kt, cache_vt, cache_lft,
                page_table, fox_w_in, fox_b_f, fox_q_gain, fox_k_gain, fox_w_out, tri)
            ks.append(new_s[0]); vs.append(new_s[1]); ls.append(new_s[2])
        elif kind == 1:
            prm = _rwkv_params(j, rwkv_mu, rwkv_w_rkvg, rwkv_w0, rwkv_w1, rwkv_w2, rwkv_a0,
                               rwkv_a1, rwkv_a2, rwkv_k_k, rwkv_k_a, rwkv_r_k, rwkv_ln_w,
                               rwkv_ln_b, rwkv_w_out)
            zero_state = jnp.zeros((bp, K_HI, HEAD_DIM, LANES), F32)
            y_p, s_fin, last = _rwkv_group(y_p, jnp.zeros((bp, 1, d), F32), zero_state, nw, prm,
                                           tm=min(tp, 256), t_valid=tp, nb=bp, tb=min(tp, 128))
            wkv_p.append(_tiles_to_state(s_fin)); sh_p.append(last)
            y_s_pad, s_fin, last = _rwkv_group(
                _pad_rows(y_s, ts_pad), state_shift[j].reshape(bs, 1, d),
                _state_to_tiles(state_wkv[j]), nw, prm, tm=ts_pad, t_valid=ts, nb=2, tb=ts)
            y_s = y_s_pad[:, :ts]
            wkv_s.append(_tiles_to_state(s_fin)); sh_s.append(last)
        else:
            w_in = pool_w_in[j].astype(BF16)
            w_grp = pool_w_grp[j].astype(BF16)
            scale = pool_scale[j].reshape(1, d)
            w_out = pool_w_out[j].astype(BF16)
            y_p, tail = _pool_mixer(y_p, jnp.zeros((bp, POOL_HIST, d), F32), nw, w_in, w_grp,
                                    scale, w_out, tm=min(tp, 256), t_valid=min(tp, 256), pos0=0)
            pool_p.append(tail[:, 1:])
            hist = jnp.pad(state_pool[j], ((0, 0), (1, 0), (0, 0)))
            y_s_pad, tail = _pool_mixer(_pad_rows(y_s, ts_pad), hist, nw, w_in, w_grp, scale,
                                        w_out, tm=ts_pad, t_valid=ts, pos0=past)
            y_s = y_s_pad[:, :ts]
            pool_s.append(tail[:, 1:])
    st = lambda xs: jnp.stack(xs, 0)
    kt_st, vt_st, lft_st = prompt_kv
    to_thd = lambda z: z.transpose(0, 1, 4, 2, 3)
    return (y_p, y_s, to_thd(kt_st), to_thd(vt_st), lft_st.transpose(0, 1, 3, 2),
            st(ks), st(vs), st(ls),
            st(wkv_p), st(sh_p), st(wkv_s), st(sh_s), st(pool_p), st(pool_s))
```
